```python
import math
import jax, jax.numpy as jnp
from jax import lax
import numpy as np


D_MODEL = 1024
BATCH = 32
SEQ = 2048
DEPTH = 1

MEM_LEN = 256
HEAD_DIM = 64
EPS = 1e-6
A_HEADS = 6
A_PATTERNS = ((128, 1), (512, 4), (2048, 16))
N_BUCKETS = 32
REL_MAX_DIST = 2048
B_HEADS = 4
B_KEY_DIM = 64
B_VAL_DIM = 96
B_CHUNK = 64
C_HEADS = 4
N_EXPERTS = 64
TOP_K = 8
N_GROUPS = 8
TOPK_GROUPS = 4
EXPERT_HIDDEN = 256
SHARED_HIDDEN = 256
ROUTED_SCALE = 2.5
MOE_BLOCK = 128

A_WIDTH = A_HEADS * HEAD_DIM
B_QK_WIDTH = B_HEADS * B_KEY_DIM
B_WIDTH = B_HEADS * B_VAL_DIM
C_WIDTH = C_HEADS * HEAD_DIM
N_BRANCHES = 3
IN_SPLITS = (A_WIDTH, A_WIDTH, A_WIDTH, B_QK_WIDTH, B_QK_WIDTH, B_WIDTH, B_WIDTH, C_WIDTH, N_BRANCHES * D_MODEL)
IN_WIDTH = sum(IN_SPLITS)

kernel_name = 'hybrid_dilated_hgrn2_memxattn_moe_block'


def split_cols(t, sizes):
    out, start = [], 0
    for s in sizes:
        out.append(t[..., start:start + s])
        start += s
    return out


def rmsnorm(t, g):
    tf = t.astype(jnp.float32)
    tf = tf * lax.rsqrt(jnp.mean(tf * tf, axis=-1, keepdims=True) + EPS)
    return (tf * g.astype(jnp.float32)).astype(t.dtype)


def t5_bucket(dist):
    d = dist.astype(jnp.int32)
    max_exact = N_BUCKETS // 2
    large = max_exact + (jnp.log(jnp.maximum(d, 1).astype(jnp.float32) / max_exact)
                         / math.log(REL_MAX_DIST / max_exact) * (N_BUCKETS - max_exact)).astype(jnp.int32)
    return jnp.where(d < max_exact, d, jnp.minimum(large, N_BUCKETS - 1))


def dilated_window_attention(q, k, v, rel_bias, window, dilation):
    B, S, H, E = q.shape
    w = window // dilation
    L = S // dilation
    nb = -(-L // w)
    Lp = nb * w

    def to_residue(t):
        t = t.reshape(B, L, dilation, H, E).transpose(0, 2, 3, 1, 4)
        return jnp.pad(t, ((0, 0), (0, 0), (0, 0), (0, Lp - L), (0, 0)))

    qs, ks, vs = to_residue(q), to_residue(k), to_residue(v)
    qb = qs.reshape(B, dilation, H, nb, w, E)

    def with_prev(t):
        tp = jnp.pad(t, ((0, 0), (0, 0), (0, 0), (w, 0), (0, 0)))
        return jnp.concatenate([tp[:, :, :, :Lp].reshape(B, dilation, H, nb, w, E),
                                t.reshape(B, dilation, H, nb, w, E)], axis=4)

    kb, vb = with_prev(ks), with_prev(vs)
    i = jnp.arange(w)[:, None]
    c = jnp.arange(2 * w)[None, :]
    steps = w + i - c
    in_band = (steps >= 0) & (steps <= w)
    blk = jnp.arange(nb)[:, None, None]
    valid = in_band[None] & ((blk > 0) | (c[None] >= w))
    bias = rel_bias.astype(jnp.float32)[t5_bucket(jnp.clip(steps, 0, w) * dilation)]
    bias = bias.transpose(2, 0, 1)
    logits = jnp.einsum('brhnie,brhnce->brhnic', qb, kb) + bias[None, None, :, None]
    logits = jnp.where(valid[None, None, None], logits, -jnp.inf)
    m = jnp.max(logits, axis=-1, keepdims=True)
    p = jnp.exp(logits - m)
    den = jnp.sum(p, axis=-1, keepdims=True)
    o = jnp.einsum('brhnic,brhnce->brhnie', p, vb) / den
    lse = (m + jnp.log(den))[..., 0]
    o = o.reshape(B, dilation, H, Lp, E)[:, :, :, :L].transpose(0, 3, 1, 2, 4).reshape(B, S, H, E)
    lse = lse.reshape(B, dilation, H, Lp)[:, :, :, :L].transpose(0, 3, 1, 2).reshape(B, S, H)
    return o, lse


def dilated_attention_branch(qa, ka, va, q_norm, k_norm, rel_bias):
    B, S, _ = qa.shape
    q = rmsnorm(qa.reshape(B, S, A_HEADS, HEAD_DIM), q_norm).astype(jnp.float32) * HEAD_DIM ** -0.5
    k = rmsnorm(ka.reshape(B, S, A_HEADS, HEAD_DIM), k_norm).astype(jnp.float32)
    v = va.reshape(B, S, A_HEADS, HEAD_DIM).astype(jnp.float32)
    outs, lses = [], []
    for window, dilation in A_PATTERNS:
        o, lse = dilated_window_attention(q, k, v, rel_bias, window, dilation)
        outs.append(o)
        lses.append(lse)
    wts = jax.nn.softmax(jnp.stack(lses, axis=0), axis=0)
    o = jnp.sum(jnp.stack(outs, axis=0) * wts[..., None], axis=0)
    return o.reshape(B, S, A_WIDTH).astype(qa.dtype)


def chunked_gated_recurrence(q, k, log_f, v):
    B, S, H, DK = q.shape
    DV = v.shape[-1]
    nc = S // B_CHUNK

    def chunks(t):
        return t.reshape(B, nc, B_CHUNK, H, t.shape[-1]).transpose(1, 0, 3, 2, 4)

    causal = jnp.tril(jnp.ones((B_CHUNK, B_CHUNK), bool))[:, :, None]

    def step(state, inp):
        qt, kt, gt, vt = inp
        b = jnp.cumsum(gt, axis=2)
        diff = jnp.where(causal, b[:, :, :, None, :] - b[:, :, None, :, :], -jnp.inf)
        scores = jnp.einsum('bhtk,bhsk,bhtsk->bhts', qt, kt, jnp.exp(diff))
        o = (jnp.einsum('bhts,bhsv->bhtv', scores, vt)
             + jnp.einsum('bhtk,bhkv->bhtv', qt * jnp.exp(b), state))
        b_last = b[:, :, -1:, :]
        state = (jnp.exp(b_last[:, :, 0, :])[..., None] * state
                 + jnp.einsum('bhsk,bhsv->bhkv', kt * jnp.exp(b_last - b), vt))
        return state, o

    state0 = jnp.zeros((B, H, DK, DV), jnp.float32)
    _, o = lax.scan(step, state0, (chunks(q), chunks(k), chunks(log_f), chunks(v)))
    return o.transpose(1, 0, 3, 2, 4).reshape(B, S, H, DV)


def hgrn2_branch(qb, fb, ib, og, lower_bound, out_norm):
    B, S, _ = qb.shape
    q = jax.nn.silu(qb.astype(jnp.float32)).reshape(B, S, B_HEADS, B_KEY_DIM) * B_KEY_DIM ** -0.5
    f = lower_bound + (1.0 - lower_bound) * jax.nn.sigmoid(fb.astype(jnp.float32))
    f = f.reshape(B, S, B_HEADS, B_KEY_DIM)
    k = 1.0 - f
    v = ib.astype(jnp.float32).reshape(B, S, B_HEADS, B_VAL_DIM)
    o = chunked_gated_recurrence(q, k, jnp.log(f), v)
    o = rmsnorm(o, out_norm.reshape(B_HEADS, B_VAL_DIM))
    o = o * jax.nn.silu(og.astype(jnp.float32)).reshape(B, S, B_HEADS, B_VAL_DIM)
    return o.reshape(B, S, B_WIDTH).astype(qb.dtype)


def memory_branch(qc, mem_n, w_mem_kv, q_norm, k_norm):
    B, S, _ = qc.shape
    M = mem_n.shape[1]
    km, vm = split_cols(mem_n @ w_mem_kv, (C_WIDTH, C_WIDTH))
    k = rmsnorm(km.reshape(B, M, C_HEADS, HEAD_DIM), k_norm).astype(jnp.float32)
    v = vm.reshape(B, M, C_HEADS, HEAD_DIM).astype(jnp.float32)
    q = rmsnorm(qc.reshape(B, S, C_HEADS, HEAD_DIM), q_norm).astype(jnp.float32) * HEAD_DIM ** -0.5
    p = jax.nn.softmax(jnp.einsum('bshe,bmhe->bhsm', q, k), axis=-1)
    o = jnp.einsum('bhsm,bmhe->bshe', p, v)
    return o.reshape(B, S, C_WIDTH).astype(qc.dtype)


def swiglu(t, w_up, w_down):
    gate, up = split_cols(t @ w_up, (w_down.shape[0], w_down.shape[0]))
    return (jax.nn.silu(gate) * up) @ w_down


def routed_experts(u, idx, wts, w_exp_up, w_exp_down):
    T, D = u.shape
    TK = T * TOP_K
    flat_e = idx.reshape(-1).astype(jnp.int32)
    order = jnp.argsort(flat_e).astype(jnp.int32)
    e_sorted = flat_e[order]
    counts = jnp.bincount(flat_e, length=N_EXPERTS).astype(jnp.int32)
    padded = (counts + MOE_BLOCK - 1) // MOE_BLOCK * MOE_BLOCK
    start = jnp.cumsum(counts) - counts
    pend = jnp.cumsum(padded)
    pstart = pend - padded
    dest = pstart[e_sorted] + jnp.arange(TK, dtype=jnp.int32) - start[e_sorted]
    n_blocks = -(-TK // MOE_BLOCK) + N_EXPERTS
    slot_of_row = jnp.full((n_blocks * MOE_BLOCK,), TK, jnp.int32).at[dest].set(order)
    block_expert = jnp.minimum(jnp.searchsorted(pend, jnp.arange(n_blocks, dtype=jnp.int32) * MOE_BLOCK,
                                                side='right'), N_EXPERTS - 1)
    tok_of_slot = jnp.concatenate([jnp.arange(TK, dtype=jnp.int32) // TOP_K, jnp.full((1,), T, jnp.int32)])
    w_of_slot = jnp.concatenate([wts.reshape(-1).astype(jnp.float32), jnp.zeros((1,), jnp.float32)])
    u_pad = jnp.concatenate([u, jnp.zeros((1, D), u.dtype)], axis=0)

    def step(acc, blk):
        slots = lax.dynamic_slice_in_dim(slot_of_row, blk * MOE_BLOCK, MOE_BLOCK)
        tok = tok_of_slot[slots]
        e = block_expert[blk]
        y = swiglu(u_pad[tok], w_exp_up[e], w_exp_down[e])
        return acc.at[tok].add(y.astype(jnp.float32) * w_of_slot[slots][:, None]), None

    acc, _ = lax.scan(step, jnp.zeros((T + 1, D), jnp.float32), jnp.arange(n_blocks, dtype=jnp.int32))
    return acc[:T]


def moe_ffn(u, w_router, router_bias, w_exp_up, w_exp_down, w_shared_up, w_shared_down):
    T = u.shape[0]
    per_group = N_EXPERTS // N_GROUPS
    s = jax.nn.sigmoid(u.astype(jnp.float32) @ w_router.astype(jnp.float32))
    choice = s + router_bias.astype(jnp.float32)
    group_score = jnp.sum(lax.top_k(choice.reshape(T, N_GROUPS, per_group), 2)[0], axis=-1)
    _, top_groups = lax.top_k(group_score, TOPK_GROUPS)
    group_mask = jnp.sum(jax.nn.one_hot(top_groups, N_GROUPS, dtype=jnp.float32), axis=1) > 0
    expert_mask = jnp.repeat(group_mask, per_group, axis=1)
    _, idx = lax.top_k(jnp.where(expert_mask, choice, -jnp.inf), TOP_K)
    w = jnp.take_along_axis(s, idx, axis=1)
    w = ROUTED_SCALE * w / jnp.sum(w, axis=-1, keepdims=True)
    routed = routed_experts(u, idx, w, w_exp_up, w_exp_down)
    shared = swiglu(u, w_shared_up, w_shared_down).astype(jnp.float32)
    return (routed + shared).astype(u.dtype)


def setup_inputs(seed: int = 0) -> dict:
    key = jax.random.key(seed)
    ks = jax.random.split(key, 24)
    f32 = jnp.float32

    def nrm(k, shape, scale):
        return jax.random.normal(k, shape, f32) * scale

    def gain(k, shape):
        return 1.0 + 0.02 * jax.random.normal(k, shape, f32)

    L = DEPTH
    return {
        'x': nrm(ks[0], (BATCH, SEQ, D_MODEL), 1.0),
        'mem': nrm(ks[1], (BATCH, MEM_LEN, D_MODEL), 1.0),
        'attn_norm_g': gain(ks[2], (L, D_MODEL)),
        'w_in': nrm(ks[3], (L, D_MODEL, IN_WIDTH), D_MODEL ** -0.5),
        'q_norm_a': gain(ks[4], (L, HEAD_DIM)),
        'k_norm_a': gain(ks[5], (L, HEAD_DIM)),
        'rel_bias': nrm(ks[6], (N_BUCKETS, A_HEADS), 0.1),
        'lb_logits': nrm(ks[7], (DEPTH + 1, B_QK_WIDTH), 0.5),
        'out_norm_b': gain(ks[8], (L, B_WIDTH)),
        'mem_norm_g': gain(ks[9], (L, D_MODEL)),
        'w_mem_kv': nrm(ks[10], (L, D_MODEL, 2 * C_WIDTH), D_MODEL ** -0.5),
        'q_norm_c': gain(ks[11], (L, HEAD_DIM)),
        'k_norm_c': gain(ks[12], (L, HEAD_DIM)),
        'w_branch_a': nrm(ks[13], (L, A_WIDTH, D_MODEL), A_WIDTH ** -0.5),
        'w_branch_b': nrm(ks[14], (L, B_WIDTH, D_MODEL), B_WIDTH ** -0.5),
        'w_branch_c': nrm(ks[15], (L, C_WIDTH, D_MODEL), C_WIDTH ** -0.5),
        'w_out': nrm(ks[16], (L, D_MODEL, D_MODEL), D_MODEL ** -0.5),
        'ffn_norm_g': gain(ks[17], (L, D_MODEL)),
        'w_router': nrm(ks[18], (L, D_MODEL, N_EXPERTS), D_MODEL ** -0.5),
        'router_bias': nrm(ks[19], (L, N_EXPERTS), 0.01),
        'w_exp_up': nrm(ks[20], (L, N_EXPERTS, D_MODEL, 2 * EXPERT_HIDDEN), D_MODEL ** -0.5),
        'w_exp_down': nrm(ks[21], (L, N_EXPERTS, EXPERT_HIDDEN, D_MODEL), EXPERT_HIDDEN ** -0.5),
        'w_shared_up': nrm(ks[22], (L, D_MODEL, 2 * SHARED_HIDDEN), D_MODEL ** -0.5),
        'w_shared_down': nrm(ks[23], (L, SHARED_HIDDEN, D_MODEL), SHARED_HIDDEN ** -0.5),
    }


def reference(x, mem, attn_norm_g, w_in, q_norm_a, k_norm_a, rel_bias, lb_logits, out_norm_b,
              mem_norm_g, w_mem_kv, q_norm_c, k_norm_c, w_branch_a, w_branch_b, w_branch_c, w_out,
              ffn_norm_g, w_router, router_bias, w_exp_up, w_exp_down, w_shared_up, w_shared_down):
    B, S, D = x.shape
    lower_bounds = jnp.cumsum(jax.nn.softmax(lb_logits.astype(jnp.float32), axis=0), axis=0)
    for layer in range(DEPTH):
        h = rmsnorm(x, attn_norm_g[layer])
        qa, ka, va, qb, fb, ib, og, qc, gate_pre = split_cols(h @ w_in[layer], IN_SPLITS)
        ya = dilated_attention_branch(qa, ka, va, q_norm_a[layer], k_norm_a[layer], rel_bias)
        yb = hgrn2_branch(qb, fb, ib, og, lower_bounds[layer], out_norm_b[layer])
        yc = memory_branch(qc, rmsnorm(mem, mem_norm_g[layer]), w_mem_kv[layer], q_norm_c[layer], k_norm_c[layer])
        g_a, g_b, g_c = split_cols(jax.nn.sigmoid(gate_pre), (D, D, D))
        mixed = (g_a * (ya @ w_branch_a[layer]) + g_b * (yb @ w_branch_b[layer])
                 + g_c * (yc @ w_branch_c[layer]))
        x = x + mixed @ w_out[layer]
        u = rmsnorm(x, ffn_norm_g[layer]).reshape(B * S, D)
        y = moe_ffn(u, w_router[layer], router_bias[layer], w_exp_up[layer], w_exp_down[layer],
                    w_shared_up[layer], w_shared_down[layer])
        x = x + y.reshape(B, S, D)
    return x
```

```python
import functools
import math

import numpy as np
import jax
import jax.numpy as jnp
from jax import lax
from jax.experimental import pallas as pl
from jax.experimental.pallas import tpu as pltpu

F32 = jnp.float32
BF16 = jnp.bfloat16
I32 = jnp.int32
U32 = jnp.uint32

HEAD_DIM = 64
EPS = 1e-6
A_HEADS = 6
A_PATTERNS = ((128, 1), (512, 4), (2048, 16))
BAND = 128
N_BUCKETS = 32
REL_MAX_DIST = 2048
B_HEADS = 4
B_KEY_DIM = 64
B_VAL_DIM = 96
B_CHUNK = 64
C_HEADS = 4
N_EXPERTS = 64
TOP_K = 8
N_GROUPS = 8
TOPK_GROUPS = 4
EXPERT_HIDDEN = 256
ROUTED_SCALE = 2.5

A_WIDTH = A_HEADS * HEAD_DIM
B_QK_WIDTH = B_HEADS * B_KEY_DIM
B_WIDTH = B_HEADS * B_VAL_DIM
C_WIDTH = C_HEADS * HEAD_DIM

NEG = -1e30
HI_MASK = 0xFFFF0000

LANES = 128
VMEM_LIMIT = 56 * 1024 * 1024

TM_PROJ = 512
TR_ROUTE = 512
TS_ROWS = 256
BM_EXPERT = 512


def _params(sem, vmem=VMEM_LIMIT):
    return pltpu.CompilerParams(dimension_semantics=sem, vmem_limit_bytes=vmem)


def _nt(a, b):
    return lax.dot_general(a, b, (((1,), (1,)), ((), ())), preferred_element_type=F32)


def _tn(a, b):
    return lax.dot_general(a, b, (((0,), (0,)), ((), ())), preferred_element_type=F32)


def _mm(a, b):
    return jnp.dot(a, b, preferred_element_type=F32)


def _split3(t):
    t1 = t.astype(BF16)
    r1 = t - t1.astype(F32)
    t2 = r1.astype(BF16)
    t3 = (r1 - t2.astype(F32)).astype(BF16)
    return t1, t2, t3


def _pack_pairs(y):
    n = y.shape[1] // 2
    bits = pltpu.bitcast(y.astype(BF16).astype(F32), U32)
    return (bits[:, :n] & U32(HI_MASK)) | (bits[:, n:] >> U32(16))


def _unpack_pairs(p):
    hi = pltpu.bitcast(p & U32(HI_MASK), F32)
    lo = pltpu.bitcast(p << U32(16), F32)
    return hi, lo


def _inproj_kernel(x_ref, g_ref, w_ref, qkv_ref, qf_ref, io_ref, qc_ref, gate_ref):
    x = x_ref[...]
    ms = jnp.mean(x * x, axis=-1, keepdims=True)
    h = (x * lax.rsqrt(ms + EPS) * g_ref[...]).astype(BF16)
    col = 0
    for out in (qkv_ref, qf_ref, io_ref, qc_ref, gate_ref):
        width = out.shape[1]
        for a in range(0, width, 512):
            b = min(a + 512, width)
            out[:, a:b] = _mm(h, w_ref[:, col + a:col + b]).astype(out.dtype)
        col += width


def _inproj(x2d, g, w_bf16):
    t, d = x2d.shape
    widths = (3 * A_WIDTH, 2 * B_QK_WIDTH, 2 * B_WIDTH, C_WIDTH, 3 * d)
    dtypes = (BF16, F32, F32, BF16, BF16)
    tm = TM_PROJ
    return pl.pallas_call(
        _inproj_kernel,
        grid=(t // tm,),
        in_specs=[
            pl.BlockSpec((tm, d), lambda i: (i, 0)),
            pl.BlockSpec((1, d), lambda i: (0, 0)),
            pl.BlockSpec(w_bf16.shape, lambda i: (0, 0), pipeline_mode=pl.Buffered(1)),
        ],
        out_specs=[pl.BlockSpec((tm, w), lambda i: (i, 0)) for w in widths],
        out_shape=[jax.ShapeDtypeStruct((t, w), dt) for w, dt in zip(widths, dtypes)],
        compiler_params=_params(("parallel",)),
        name="inproj",
    )(x2d, g.reshape(1, d), w_bf16)


def _bucket_tables():
    w = BAND
    i = np.arange(w)[:, None]
    c = np.arange(2 * w)[None, :]
    steps = w + i - c
    in_band = (steps >= 0) & (steps <= w)
    max_exact = N_BUCKETS // 2
    tabs = []
    for _, dil in A_PATTERNS:
        d = (np.clip(steps, 0, w) * dil).astype(np.int32)
        ratio = np.maximum(d, 1).astype(np.float32) / np.float32(max_exact)
        large = max_exact + (np.log(ratio) / np.float32(math.log(REL_MAX_DIST / max_exact))
                             * np.float32(N_BUCKETS - max_exact)).astype(np.int32)
        bucket = np.where(d < max_exact, d, np.minimum(large, N_BUCKETS - 1))
        tabs.append(np.where(in_band, bucket, -1))
    return np.stack(tabs).astype(np.int32)


def _bias_kernel(rb_ref, bucket_ref, out_ref):
    h = pl.program_id(1)
    bk = bucket_ref[0]
    acc = jnp.full(bk.shape, NEG, F32)
    for b in range(N_BUCKETS):
        acc = jnp.where(bk == b, rb_ref[b, h], acc)
    out_ref[0, 0] = acc


def _bias_tables(rel_bias):
    buckets = jnp.asarray(_bucket_tables())
    p = len(A_PATTERNS)
    return pl.pallas_call(
        _bias_kernel,
        grid=(p, A_HEADS),
        in_specs=[
            pl.BlockSpec(memory_space=pltpu.SMEM),
            pl.BlockSpec((1, BAND, 2 * BAND), lambda i, h: (i, 0, 0)),
        ],
        out_specs=pl.BlockSpec((1, 1, BAND, 2 * BAND), lambda i, h: (i, h, 0, 0)),
        out_shape=jax.ShapeDtypeStruct((p, A_HEADS, BAND, 2 * BAND), F32),
        compiler_params=_params(("arbitrary", "arbitrary")),
        name="attn_bias",
    )(rel_bias.astype(F32), buckets)


def _attn_kernel(q_ref, k_ref, v_ref, qg_ref, kg_ref, tab_ref, o_ref,
                 qn, kn, vf, qp, kp0, kp1, vp, mrun, lrun, arun):
    s = q_ref.shape[0]
    w = BAND
    lane = lax.broadcasted_iota(I32, (1, LANES), 1)
    h0 = lane < HEAD_DIM
    rows = 256

    def norm_body(c, carry):
        r0 = pl.multiple_of(c * rows, rows)
        for src, dst, g in ((q_ref, qn, qg_ref), (k_ref, kn, kg_ref)):
            t = src[pl.ds(r0, rows), :].astype(F32)
            sq = t * t
            s0 = jnp.sum(jnp.where(h0, sq, 0.0), axis=-1, keepdims=True)
            s1 = jnp.sum(jnp.where(h0, 0.0, sq), axis=-1, keepdims=True)
            ms = jnp.where(h0, s0, s1) * (1.0 / HEAD_DIM)
            dst[pl.ds(r0, rows), :] = t * lax.rsqrt(ms + EPS) * g[...]
        vf[pl.ds(r0, rows), :] = v_ref[pl.ds(r0, rows), :].astype(F32)
        return carry

    lax.fori_loop(0, s // rows, norm_body, 0)

    mrun[...] = jnp.full(mrun.shape, NEG, F32)
    lrun[...] = jnp.zeros(lrun.shape, F32)
    arun[...] = jnp.zeros(arun.shape, F32)
    zpad = jnp.zeros((w, LANES), BF16)
    kp0[0:w, :] = zpad
    kp1[0:w, :] = zpad
    vp[0:w, :] = zpad

    col = lax.broadcasted_iota(I32, (1, 2 * w), 1)

    for p, (window, dil) in enumerate(A_PATTERNS):
        seg = s // dil
        nb = seg // w
        nblk = s // w

        def src_index(start):
            if dil == 1:
                return pl.ds(start, w)
            return pl.ds(start, w, stride=dil)

        for r in range(dil):
            for c in range(nb):
                idx = src_index(r + dil * w * c)
                dst = w + r * seg + c * w
                kv = kn[idx, :]
                qp[dst:dst + w, :] = qn[idx, :].astype(BF16)
                kp0[dst:dst + w, :] = jnp.where(h0, kv, 0.0).astype(BF16)
                kp1[dst:dst + w, :] = jnp.where(h0, 0.0, kv).astype(BF16)
                vp[dst:dst + w, :] = vf[idx, :].astype(BF16)

        def blk(i, carry):
            row0 = pl.multiple_of(i * w, w)
            qb = qp[pl.ds(row0 + w, w), :]
            vw = vp[pl.ds(row0, 2 * w), :]
            n = lax.rem(i, nb)
            r = lax.div(i, nb)
            ok = jnp.logical_or(col >= w, n > 0)
            res = []
            for h, kp in ((0, kp0), (1, kp1)):
                kw = kp[pl.ds(row0, 2 * w), :]
                lg = _nt(qb, kw) + tab_ref[p, h]
                lg = jnp.where(ok, lg, NEG)
                m = jnp.max(lg, axis=-1, keepdims=True)
                pe = jnp.exp(lg - m)
                l = jnp.sum(pe, axis=-1, keepdims=True)
                pv = _mm(pe.astype(BF16), vw)
                res.append((m, l, pv))
            mc = jnp.where(h0, res[0][0], res[1][0])
            lc = jnp.where(h0, res[0][1], res[1][1])
            ac = jnp.where(h0, res[0][2], res[1][2])
            tok = src_index(r + dil * w * n)
            mo = mrun[tok, :]
            mn = jnp.maximum(mo, mc)
            ea = jnp.exp(mo - mn)
            eb = jnp.exp(mc - mn)
            lrun[tok, :] = ea * lrun[tok, :] + eb * lc
            arun[tok, :] = ea * arun[tok, :] + eb * ac
            mrun[tok, :] = mn
            return carry

        lax.fori_loop(0, nblk, blk, 0)

    def out_body(c, carry):
        r0 = pl.multiple_of(c * rows, rows)
        o_ref[pl.ds(r0, rows), :] = (arun[pl.ds(r0, rows), :] / lrun[pl.ds(r0, rows), :]).astype(o_ref.dtype)
        return carry

    lax.fori_loop(0, s // rows, out_body, 0)


def _attention(qkv, q_gain, k_gain, tab, batch, seq):
    t = qkv.shape[0]
    npair = A_HEADS // 2
    qg = (jnp.tile(q_gain.astype(F32), 2) * HEAD_DIM ** -0.5).reshape(1, LANES)
    kg = jnp.tile(k_gain.astype(F32), 2).reshape(1, LANES)
    spad = seq + BAND
    return pl.pallas_call(
        _attn_kernel,
        grid=(batch, npair),
        in_specs=[
            pl.BlockSpec((seq, LANES), lambda b, h: (b, h)),
            pl.BlockSpec((seq, LANES), lambda b, h: (b, npair + h)),
            pl.BlockSpec((seq, LANES), lambda b, h: (b, 2 * npair + h)),
            pl.BlockSpec((1, LANES), lambda b, h: (0, 0)),
            pl.BlockSpec((1, LANES), lambda b, h: (0, 0)),
            pl.BlockSpec((len(A_PATTERNS), 2, BAND, 2 * BAND), lambda b, h: (0, h, 0, 0)),
        ],
        out_specs=pl.BlockSpec((seq, LANES), lambda b, h: (b, h)),
        out_shape=jax.ShapeDtypeStruct((t, A_WIDTH), BF16),
        scratch_shapes=[
            pltpu.VMEM((seq, LANES), F32),
            pltpu.VMEM((seq, LANES), F32),
            pltpu.VMEM((seq, LANES), F32),
            pltpu.VMEM((spad, LANES), BF16),
            pltpu.VMEM((spad, LANES), BF16),
            pltpu.VMEM((spad, LANES), BF16),
            pltpu.VMEM((spad, LANES), BF16),
            pltpu.VMEM((seq, LANES), F32),
            pltpu.VMEM((seq, LANES), F32),
            pltpu.VMEM((seq, LANES), F32),
        ],
        compiler_params=_params(("parallel", "parallel")),
        name="dilated_attn",
    )(qkv, qkv, qkv, qg, kg, tab)


def _hgrn_consts():
    c = B_CHUNK
    hk = B_QK_WIDTH
    hv = B_WIDTH
    grp = 16
    tri = np.tril(np.ones((c, c), np.float32))
    gt = np.zeros((B_HEADS * grp, grp * hk), np.float32)
    for h in range(B_HEADS):
        for j in range(grp):
            gt[h * grp + j, j * hk + h * B_KEY_DIM:j * hk + (h + 1) * B_KEY_DIM] = 1.0
    bd = np.zeros((hv, hk), np.float32)
    vm = np.zeros((B_HEADS * grp, hv), np.float32)
    hm = np.zeros((hv, hv), np.float32)
    for h in range(B_HEADS):
        bd[h * B_VAL_DIM:(h + 1) * B_VAL_DIM, h * B_KEY_DIM:(h + 1) * B_KEY_DIM] = 1.0
        vm[h * grp:(h + 1) * grp, h * B_VAL_DIM:(h + 1) * B_VAL_DIM] = 1.0
        hm[h * B_VAL_DIM:(h + 1) * B_VAL_DIM, h * B_VAL_DIM:(h + 1) * B_VAL_DIM] = 1.0
    return tri, gt, bd, vm, hm, grp


def _hgrn_kernel(qb_ref, fb_ref, ib_ref, og_ref, lbl_ref, onorm_ref, tri_ref, gt_ref, bd_ref, vm_ref, hm_ref,
                 o_ref, st, dstack, bsc, qsc, ksc, *, grp, layer):
    s = qb_ref.shape[0]
    c = B_CHUNK
    hk = B_QK_WIDTH
    ngrp = c // grp

    st[...] = jnp.zeros(st.shape, F32)
    dstack[...] = jnp.zeros(dstack.shape, BF16)
    lbl = lbl_ref[...]
    e = jnp.exp(lbl - jnp.max(lbl, axis=0, keepdims=True))
    lb = jnp.sum(e[0:layer + 1, :], axis=0, keepdims=True) / jnp.sum(e, axis=0, keepdims=True)
    tri = tri_ref[...]

    def chunk(ci, carry):
        c0 = pl.multiple_of(ci * c, c)
        qr = qb_ref[pl.ds(c0, c), :]
        q = qr * jax.nn.sigmoid(qr) * (B_KEY_DIM ** -0.5)
        f = lb + (1.0 - lb) * jax.nn.sigmoid(fb_ref[pl.ds(c0, c), :])
        g1, g2, g3 = _split3(jnp.log(f))
        bcum = _mm(tri, g1) + _mm(tri, g2) + _mm(tri, g3)
        kk = 1.0 - f
        v = ib_ref[pl.ds(c0, c), :]
        vb = v.astype(BF16)

        stv = st[...]
        o = _nt((q * jnp.exp(bcum)).astype(BF16), stv.astype(BF16))
        blast = bcum[c - 1:c, :]
        kd = kk * jnp.exp(blast - bcum)
        st[...] = stv * jnp.exp(blast) + _tn(vb, kd.astype(BF16)) * bd_ref[...]

        bsc[...] = bcum
        qsc[...] = q
        ksc[...] = kk
        for sp in range(c):
            t0 = (sp // 16) * 16
            n = c - t0
            dif = bsc[t0:c, :] - bsc[sp:sp + 1, :]
            trow = lax.broadcasted_iota(I32, (n, 1), 0) + t0
            ex = jnp.exp(jnp.where(trow >= sp, dif, NEG))
            dv = (qsc[t0:c, :] * ksc[sp:sp + 1, :]) * ex
            dstack[t0:c, sp * hk:(sp + 1) * hk] = dv.astype(BF16)

        for gi in range(ngrp):
            sc = _nt(dstack[:, gi * grp * hk:(gi + 1) * grp * hk], gt_ref[...])
            vg = v[gi * grp:(gi + 1) * grp, :]
            vg = (jnp.concatenate([vg] * B_HEADS, axis=0) * vm_ref[...]).astype(BF16)
            o = o + _mm(sc.astype(BF16), vg)

        o2 = o * o
        o2h = o2.astype(BF16)
        o2l = (o2 - o2h.astype(F32)).astype(BF16)
        ms = (_mm(o2h, hm_ref[...]) + _mm(o2l, hm_ref[...])) * (1.0 / B_VAL_DIM)
        ogr = og_ref[pl.ds(c0, c), :]
        y = o * lax.rsqrt(ms + EPS) * onorm_ref[...] * (ogr * jax.nn.sigmoid(ogr))
        o_ref[pl.ds(c0, c), :] = y.astype(o_ref.dtype)
        return carry

    lax.fori_loop(0, s // c, chunk, 0)


def _hgrn(qf, io, lb_logits, layer, out_norm, batch, seq):
    t = qf.shape[0]
    tri, gt, bd, vm, hm, grp = _hgrn_consts()
    nslot = lb_logits.shape[0]
    const = lambda a, dt: jnp.asarray(a, dt)
    full = lambda shp: pl.BlockSpec(shp, lambda b: (0,) * len(shp))
    return pl.pallas_call(
        functools.partial(_hgrn_kernel, grp=grp, layer=layer),
        grid=(batch,),
        in_specs=[
            pl.BlockSpec((seq, B_QK_WIDTH), lambda b: (b, 0)),
            pl.BlockSpec((seq, B_QK_WIDTH), lambda b: (b, 1)),
            pl.BlockSpec((seq, B_WIDTH), lambda b: (b, 0)),
            pl.BlockSpec((seq, B_WIDTH), lambda b: (b, 1)),
            full((nslot, B_QK_WIDTH)),
            full((1, B_WIDTH)),
            full(tri.shape), full(gt.shape), full(bd.shape), full(vm.shape), full(hm.shape),
        ],
        out_specs=pl.BlockSpec((seq, B_WIDTH), lambda b: (b, 0)),
        out_shape=jax.ShapeDtypeStruct((t, B_WIDTH), BF16),
        scratch_shapes=[
            pltpu.VMEM((B_WIDTH, B_QK_WIDTH), F32),
            pltpu.VMEM((B_CHUNK, B_CHUNK * B_QK_WIDTH), BF16),
            pltpu.VMEM((B_CHUNK, B_QK_WIDTH), F32),
            pltpu.VMEM((B_CHUNK, B_QK_WIDTH), F32),
            pltpu.VMEM((B_CHUNK, B_QK_WIDTH), F32),
        ],
        compiler_params=_params(("parallel",)),
        name="hgrn2",
    )(qf, qf, io, io, lb_logits.astype(F32), out_norm.astype(F32).reshape(1, B_WIDTH),
      const(tri, BF16), const(gt, BF16), const(bd, F32), const(vm, F32), const(hm, BF16))


def _head_ms(t, masks):
    sq = t * t
    ms = jnp.zeros_like(t)
    for mk in masks:
        ms = jnp.where(mk, jnp.sum(jnp.where(mk, sq, 0.0), axis=-1, keepdims=True), ms)
    return ms * (1.0 / HEAD_DIM)


def _mem_kernel(mem_ref, mg_ref, wkv_ref, q_ref, qg_ref, kg_ref, o_ref, ksc, vsc):
    s = q_ref.shape[0]
    lane = lax.broadcasted_iota(I32, (1, C_WIDTH), 1)
    masks = [(lane >= h * HEAD_DIM) & (lane < (h + 1) * HEAD_DIM) for h in range(C_HEADS)]

    m = mem_ref[0]
    ms = jnp.mean(m * m, axis=-1, keepdims=True)
    mn = (m * lax.rsqrt(ms + EPS) * mg_ref[...]).astype(BF16)
    kv = _mm(mn, wkv_ref[...])
    km = kv[:, :C_WIDTH]
    k = km * lax.rsqrt(_head_ms(km, masks) + EPS) * kg_ref[...]
    for h in range(C_HEADS):
        ksc[h] = jnp.where(masks[h], k, 0.0).astype(BF16)
    vsc[...] = kv[:, C_WIDTH:].astype(BF16)

    rows = 256

    def body(c, carry):
        r0 = pl.multiple_of(c * rows, rows)
        q = q_ref[pl.ds(r0, rows), :].astype(F32)
        qn = (q * lax.rsqrt(_head_ms(q, masks) + EPS) * qg_ref[...]).astype(BF16)
        out = jnp.zeros((rows, C_WIDTH), F32)
        for h in range(C_HEADS):
            lg = _nt(qn, ksc[h])
            mx = jnp.max(lg, axis=-1, keepdims=True)
            pe = jnp.exp(lg - mx)
            l = jnp.sum(pe, axis=-1, keepdims=True)
            out = jnp.where(masks[h], _mm(pe.astype(BF16), vsc[...]) / l, out)
        o_ref[pl.ds(r0, rows), :] = out.astype(o_ref.dtype)
        return carry

    lax.fori_loop(0, s // rows, body, 0)


def _memory(mem, mem_g, wkv_bf16, qc, q_gain, k_gain, seq):
    batch, mlen, d = mem.shape
    t = qc.shape[0]
    qg = (jnp.tile(q_gain.astype(F32), C_HEADS) * HEAD_DIM ** -0.5).reshape(1, C_WIDTH)
    kg = jnp.tile(k_gain.astype(F32), C_HEADS).reshape(1, C_WIDTH)
    return pl.pallas_call(
        _mem_kernel,
        grid=(batch,),
        in_specs=[
            pl.BlockSpec((1, mlen, d), lambda b: (b, 0, 0)),
            pl.BlockSpec((1, d), lambda b: (0, 0)),
            pl.BlockSpec(wkv_bf16.shape, lambda b: (0, 0)),
            pl.BlockSpec((seq, C_WIDTH), lambda b: (b, 0)),
            pl.BlockSpec((1, C_WIDTH), lambda b: (0, 0)),
            pl.BlockSpec((1, C_WIDTH), lambda b: (0, 0)),
        ],
        out_specs=pl.BlockSpec((seq, C_WIDTH), lambda b: (b, 0)),
        out_shape=jax.ShapeDtypeStruct((t, C_WIDTH), BF16),
        scratch_shapes=[
            pltpu.VMEM((C_HEADS, mlen, C_WIDTH), BF16),
            pltpu.VMEM((mlen, C_WIDTH), BF16),
        ],
        compiler_params=_params(("parallel",)),
        name="mem_xattn",
    )(mem, mem_g.astype(F32).reshape(1, d), wkv_bf16, qc, qg, kg)


def _merge_kernel(x_ref, ya_ref, yb_ref, yc_ref, gate_ref, wa_ref, wb_ref, wc_ref, wo_ref, fg_ref,
                  wrh_ref, wrl_ref, wsu_ref, wsd_ref, x2_ref, up_ref, st_ref):
    d = x_ref.shape[1]
    mixed = jnp.zeros(x_ref.shape, F32)
    for i, (y_ref, w_ref) in enumerate(((ya_ref, wa_ref), (yb_ref, wb_ref), (yc_ref, wc_ref))):
        g = jax.nn.sigmoid(gate_ref[:, i * d:(i + 1) * d].astype(F32))
        mixed = mixed + g * _mm(y_ref[...], w_ref[...])
    x1 = x_ref[...] + _mm(mixed.astype(BF16), wo_ref[...])
    ms = jnp.mean(x1 * x1, axis=-1, keepdims=True)
    u = x1 * lax.rsqrt(ms + EPS) * fg_ref[...]
    ub = u.astype(BF16)
    ul = (u - ub.astype(F32)).astype(BF16)
    logits = _nt(wrh_ref[...], ub) + _nt(wrl_ref[...], ub) + _nt(wrh_ref[...], ul)
    st_ref[...] = jax.nn.sigmoid(logits)
    hid = _mm(ub, wsu_ref[...])
    nh = hid.shape[1] // 2
    act = (jax.nn.silu(hid[:, :nh]) * hid[:, nh:]).astype(BF16)
    x2_ref[...] = x1 + _mm(act, wsd_ref[...])
    up_ref[...] = _pack_pairs(ub.astype(F32))


def _merge(x2d, ya, yb, yc, gates, wa, wb, wc, wo, fg, wr_hi, wr_lo, wsu, wsd):
    t, d = x2d.shape
    tm = TM_PROJ
    row = lambda w: pl.BlockSpec((tm, w), lambda i: (i, 0))
    full = lambda a: pl.BlockSpec(a.shape, lambda i: (0,) * a.ndim)
    fg2 = fg.astype(F32).reshape(1, d)
    return pl.pallas_call(
        _merge_kernel,
        grid=(t // tm,),
        in_specs=[row(d), row(A_WIDTH), row(B_WIDTH), row(C_WIDTH), row(3 * d),
                  full(wa), full(wb), full(wc), full(wo), full(fg2), full(wr_hi), full(wr_lo),
                  full(wsu), full(wsd)],
        out_specs=[row(d), row(d // 2), pl.BlockSpec((N_EXPERTS, tm), lambda i: (0, i))],
        out_shape=[jax.ShapeDtypeStruct((t, d), F32),
                   jax.ShapeDtypeStruct((t, d // 2), U32),
                   jax.ShapeDtypeStruct((N_EXPERTS, t), F32)],
        compiler_params=_params(("parallel",)),
        name="merge_router_shared",
    )(x2d, ya, yb, yc, gates, wa, wb, wc, wo, fg2, wr_hi, wr_lo, wsu, wsd)


def _row_to_expert():
    per_group = N_EXPERTS // N_GROUPS
    rows = np.arange(N_EXPERTS)
    return (rows % N_GROUPS) * per_group + rows // N_GROUPS


def _route_kernel(s_ref, bias_ref, eidx_ref, ut_ref, ones_ref, idx_ref, w_ref, rank_ref, cnt_ref, carry):
    tr = s_ref.shape[1]
    per_group = N_EXPERTS // N_GROUPS
    ng = N_GROUPS
    nrep = tr // LANES
    ninf = -jnp.inf

    @pl.when(pl.program_id(0) == 0)
    def _():
        carry[...] = jnp.zeros(carry.shape, F32)

    s_all = s_ref[...]
    bias = jnp.concatenate([bias_ref[...]] * nrep, axis=1)
    eidx = jnp.concatenate([eidx_ref[...]] * nrep, axis=1)
    sv = [s_all[i * ng:(i + 1) * ng, :] for i in range(per_group)]
    ev = [eidx[i * ng:(i + 1) * ng, :] for i in range(per_group)]
    cv = [sv[i] + bias[i * ng:(i + 1) * ng, :] for i in range(per_group)]

    def vmax(xs):
        out = xs[0]
        for x in xs[1:]:
            out = jnp.maximum(out, x)
        return out

    def vmin(xs):
        out = xs[0]
        for x in xs[1:]:
            out = jnp.minimum(out, x)
        return out

    m1 = vmax(cv)
    i1 = vmin([jnp.where(cv[i] == m1, i, per_group) for i in range(per_group)])
    m2 = vmax([jnp.where(i1 == i, ninf, cv[i]) for i in range(per_group)])
    gs = m1 + m2
    giota = lax.broadcasted_iota(I32, (ng, tr), 0)
    gmask = jnp.zeros((ng, tr), jnp.bool_)
    for _ in range(TOPK_GROUPS):
        mx = jnp.max(gs, axis=0, keepdims=True)
        pick = jnp.min(jnp.where(gs == mx, giota, ng), axis=0, keepdims=True)
        sel = giota == pick
        gmask = jnp.logical_or(gmask, sel)
        gs = jnp.where(sel, ninf, gs)

    cm = [jnp.where(gmask, cv[i], ninf) for i in range(per_group)]
    idx_out = jnp.zeros((TOP_K, tr), I32)
    w_out = jnp.zeros((TOP_K, tr), F32)
    jiota = lax.broadcasted_iota(I32, (TOP_K, tr), 0)
    for j in range(TOP_K):
        mx = jnp.max(vmax(cm), axis=0, keepdims=True)
        emin = jnp.min(vmin([jnp.where(cm[i] == mx, ev[i], N_EXPERTS) for i in range(per_group)]),
                       axis=0, keepdims=True)
        hit = [ev[i] == emin for i in range(per_group)]
        ssel = hit_sum = None
        for i in range(per_group):
            term = jnp.where(hit[i], sv[i], 0.0)
            hit_sum = term if hit_sum is None else hit_sum + term
        ssel = jnp.sum(hit_sum, axis=0, keepdims=True)
        cm = [jnp.where(hit[i], ninf, cm[i]) for i in range(per_group)]
        idx_out = jnp.where(jiota == j, emin, idx_out)
        w_out = jnp.where(jiota == j, ssel, w_out)
    idx_ref[...] = idx_out
    w_ref[...] = ROUTED_SCALE * w_out / jnp.sum(w_out, axis=0, keepdims=True)

    chosen = [jnp.where(jnp.logical_and(gmask, cm[i] == ninf), 1.0, 0.0) for i in range(per_group)]
    mh = jnp.concatenate(chosen, axis=0).astype(BF16)
    cr = carry[...]
    rank_full = _mm(mh, ut_ref[...]) + jnp.concatenate([cr] * nrep, axis=1)
    rv = [rank_full[i * ng:(i + 1) * ng, :] for i in range(per_group)]
    rank_out = jnp.zeros((TOP_K, tr), F32)
    for j in range(TOP_K):
        ej = idx_out[j:j + 1, :]
        acc = None
        for i in range(per_group):
            term = jnp.where(ev[i] == ej, rv[i], 0.0)
            acc = term if acc is None else acc + term
        rank_out = jnp.where(jiota == j, jnp.sum(acc, axis=0, keepdims=True), rank_out)
    rank_ref[...] = rank_out.astype(I32)
    carry[...] = cr + _mm(mh, ones_ref[...])
    cnt_ref[...] = carry[...]


def _route(s_t, bias_rows, t):
    tr = TR_ROUTE
    r2e = _row_to_expert()
    eidx = jnp.asarray(np.broadcast_to(r2e[:, None], (N_EXPERTS, LANES)).astype(np.int32))
    bias = jnp.broadcast_to(bias_rows.astype(F32)[:, None], (N_EXPERTS, LANES))
    ut = jnp.asarray(np.triu(np.ones((tr, tr), np.float32), k=1), BF16)
    ones = jnp.ones((tr, LANES), BF16)
    full = lambda a: pl.BlockSpec(a.shape, lambda i: (0,) * a.ndim)
    tile = pl.BlockSpec((TOP_K, tr), lambda i: (0, i))
    return pl.pallas_call(
        _route_kernel,
        grid=(t // tr,),
        in_specs=[pl.BlockSpec((N_EXPERTS, tr), lambda i: (0, i)), full(bias), full(eidx), full(ut), full(ones)],
        out_specs=[tile, tile, tile, pl.BlockSpec((N_EXPERTS, LANES), lambda i: (0, 0))],
        out_shape=[jax.ShapeDtypeStruct((TOP_K, t), I32),
                   jax.ShapeDtypeStruct((TOP_K, t), F32),
                   jax.ShapeDtypeStruct((TOP_K, t), I32),
                   jax.ShapeDtypeStruct((N_EXPERTS, LANES), F32)],
        scratch_shapes=[pltpu.VMEM((N_EXPERTS, LANES), F32)],
        compiler_params=_params(("arbitrary",)),
        name="route_topk_rank",
    )(s_t, bias, eidx, ut, ones)


def _dest_kernel(ps_ref, idx_ref, rank_ref, dest_ref):
    idx = idx_ref[...]
    acc = rank_ref[...]
    for e in range(N_EXPERTS):
        acc = acc + jnp.where(idx == e, ps_ref[e], 0)
    dest_ref[0] = acc


def _dest_rows(pstart, idx_t, rank_t, ts):
    t = idx_t.shape[1]
    tile = pl.BlockSpec((TOP_K, ts), lambda i, ps: (0, i))
    return pl.pallas_call(
        _dest_kernel,
        grid_spec=pltpu.PrefetchScalarGridSpec(
            num_scalar_prefetch=1,
            grid=(t // ts,),
            in_specs=[tile, tile],
            out_specs=pl.BlockSpec((1, TOP_K, ts), lambda i, ps: (i, 0, 0)),
        ),
        out_shape=jax.ShapeDtypeStruct((t // ts, TOP_K, ts), I32),
        compiler_params=_params(("arbitrary",)),
        name="dest_rows",
    )(pstart, idx_t, rank_t)


def _dispatch_kernel(zrow_ref, dest_hbm, u_ref, xs_hbm, dsm, zbuf, sem, zsem, isem):
    i = pl.program_id(0)
    nt = pl.num_programs(0)
    ts = u_ref.shape[0]
    bm = zbuf.shape[0]

    def idx_copy(tile, slot):
        return pltpu.make_async_copy(dest_hbm.at[tile], dsm.at[slot], isem.at[slot])

    def zero_copy(e):
        return pltpu.make_async_copy(zbuf, xs_hbm.at[pl.ds(pl.multiple_of(zrow_ref[e], bm), bm)], zsem)

    @pl.when(i == 0)
    def _():
        idx_copy(0, 0).start()
        zbuf[...] = jnp.zeros(zbuf.shape, zbuf.dtype)

        def zstart(e, carry):
            @pl.when(zrow_ref[e] >= 0)
            def _():
                zero_copy(e).start()
            return carry

        def zwait(e, carry):
            @pl.when(zrow_ref[e] >= 0)
            def _():
                zero_copy(e).wait()
            return carry

        lax.fori_loop(0, N_EXPERTS, zstart, 0)
        lax.fori_loop(0, N_EXPERTS, zwait, 0)

    slot = lax.rem(i, 2)
    idx_copy(i, slot).wait()

    @pl.when(i + 1 < nt)
    def _():
        idx_copy(i + 1, 1 - slot).start()

    def row_copy(t, j):
        return pltpu.make_async_copy(u_ref.at[t], xs_hbm.at[dsm[slot, j, t]], sem)

    def issue(t, carry):
        for j in range(TOP_K):
            row_copy(t, j).start()
        return carry

    def drain(t, carry):
        for j in range(TOP_K):
            row_copy(t, j).wait()
        return carry

    lax.fori_loop(0, ts, issue, 0)
    lax.fori_loop(0, ts, drain, 0)


def _dispatch(zrow, dest_tiles, u_packed, n_rows):
    t, half = u_packed.shape
    ts = TS_ROWS
    return pl.pallas_call(
        _dispatch_kernel,
        grid_spec=pltpu.PrefetchScalarGridSpec(
            num_scalar_prefetch=1,
            grid=(t // ts,),
            in_specs=[pl.BlockSpec(memory_space=pl.ANY),
                      pl.BlockSpec((ts, half), lambda i, z: (i, 0))],
            out_specs=pl.BlockSpec(memory_space=pl.ANY),
            scratch_shapes=[
                pltpu.SMEM((2, TOP_K, ts), I32),
                pltpu.VMEM((BM_EXPERT, half), U32),
                pltpu.SemaphoreType.DMA,
                pltpu.SemaphoreType.DMA,
                pltpu.SemaphoreType.DMA((2,)),
            ],
        ),
        out_shape=jax.ShapeDtypeStruct((n_rows, half), U32),
        compiler_params=_params(("arbitrary",)),
        name="dispatch_rows",
    )(zrow, dest_tiles, u_packed)


def _expert_kernel(be_ref, nu_ref, xs_ref, wu_ref, wd_ref, ys_ref):
    @pl.when(pl.program_id(0) < nu_ref[0])
    def _():
        xa, xb = _unpack_pairs(xs_ref[...])
        half = xa.shape[1]
        hid = _mm(xa.astype(BF16), wu_ref[0, :half, :]) + _mm(xb.astype(BF16), wu_ref[0, half:, :])
        nh = hid.shape[1] // 2
        act = (jax.nn.silu(hid[:, :nh]) * hid[:, nh:]).astype(BF16)
        ys_ref[...] = _pack_pairs(_mm(act, wd_ref[0]))


def _experts(block_expert, n_used, xs, wu, wd):
    n_rows, half = xs.shape
    bm = BM_EXPERT
    nblk = n_rows // bm
    d = 2 * half
    last = lambda b, nu: jnp.minimum(b, nu[0] - 1)
    return pl.pallas_call(
        _expert_kernel,
        grid_spec=pltpu.PrefetchScalarGridSpec(
            num_scalar_prefetch=2,
            grid=(nblk,),
            in_specs=[
                pl.BlockSpec((bm, half), lambda b, be, nu: (last(b, nu), 0)),
                pl.BlockSpec((1, d, wu.shape[2]), lambda b, be, nu: (be[last(b, nu)], 0, 0)),
                pl.BlockSpec((1, wd.shape[1], d), lambda b, be, nu: (be[last(b, nu)], 0, 0)),
            ],
            out_specs=pl.BlockSpec((bm, half), lambda b, be, nu: (last(b, nu), 0)),
        ),
        out_shape=jax.ShapeDtypeStruct((n_rows, half), U32),
        compiler_params=_params(("arbitrary",)),
        name="expert_swiglu",
    )(block_expert, n_used, xs, wu, wd)


def _combine_kernel(dest_hbm, ys_hbm, x2_ref, w_ref, o_ref, dsm, buf, sems, isem):
    i = pl.program_id(0)
    nt = pl.num_programs(0)
    ts = x2_ref.shape[0]
    half = buf.shape[3]

    def idx_copy(tile, slot):
        return pltpu.make_async_copy(dest_hbm.at[tile], dsm.at[slot], isem.at[slot])

    def row_copy(slot, t, j):
        return pltpu.make_async_copy(ys_hbm.at[dsm[slot, j, t]], buf.at[slot, j, t], sems.at[slot])

    def issue(slot):
        def body(t, carry):
            for j in range(TOP_K):
                row_copy(slot, t, j).start()
            return carry
        lax.fori_loop(0, ts, body, 0)

    slot = lax.rem(i, 2)

    @pl.when(i == 0)
    def _():
        idx_copy(0, 0).start()
        idx_copy(0, 0).wait()
        issue(0)

    @pl.when(i + 1 < nt)
    def _():
        idx_copy(i + 1, 1 - slot).start()
        idx_copy(i + 1, 1 - slot).wait()
        issue(1 - slot)

    def drain(t, carry):
        for j in range(TOP_K):
            row_copy(slot, t, j).wait()
        return carry

    lax.fori_loop(0, ts, drain, 0)

    acc_a = x2_ref[:, :half]
    acc_b = x2_ref[:, half:]
    for j in range(TOP_K):
        ya, yb = _unpack_pairs(buf[slot, j])
        wj = w_ref[:, j:j + 1]
        acc_a = acc_a + wj * ya
        acc_b = acc_b + wj * yb
    o_ref[:, :half] = acc_a
    o_ref[:, half:] = acc_b


def _combine(dest_tiles, ys, x2, w_tok):
    t, d = x2.shape
    ts = TS_ROWS
    half = d // 2
    return pl.pallas_call(
        _combine_kernel,
        grid=(t // ts,),
        in_specs=[pl.BlockSpec(memory_space=pl.ANY),
                  pl.BlockSpec(memory_space=pl.ANY),
                  pl.BlockSpec((ts, d), lambda i: (i, 0)),
                  pl.BlockSpec((ts, TOP_K), lambda i: (i, 0))],
        out_specs=pl.BlockSpec((ts, d), lambda i: (i, 0)),
        out_shape=jax.ShapeDtypeStruct((t, d), F32),
        scratch_shapes=[
            pltpu.SMEM((2, TOP_K, ts), I32),
            pltpu.VMEM((2, TOP_K, ts, half), U32),
            pltpu.SemaphoreType.DMA((2,)),
            pltpu.SemaphoreType.DMA((2,)),
        ],
        compiler_params=_params(("arbitrary",)),
        name="combine_rows",
    )(dest_tiles, ys, x2, w_tok)


def kernel(x, mem, attn_norm_g, w_in, q_norm_a, k_norm_a, rel_bias, lb_logits, out_norm_b, mem_norm_g, w_mem_kv,
           q_norm_c, k_norm_c, w_branch_a, w_branch_b, w_branch_c, w_out, ffn_norm_g, w_router, router_bias,
           w_exp_up, w_exp_down, w_shared_up, w_shared_down):
    batch, seq, d = x.shape
    t = batch * seq
    depth = w_in.shape[0]
    tab = _bias_tables(rel_bias)
    r2e = _row_to_expert()
    xf = x.reshape(t, d)
    for layer in range(depth):
        qkv, qf, io, qc, gates = _inproj(xf, attn_norm_g[layer], w_in[layer].astype(BF16))
        ya = _attention(qkv, q_norm_a[layer], k_norm_a[layer], tab, batch, seq)
        yb = _hgrn(qf, io, lb_logits, layer, out_norm_b[layer], batch, seq)
        yc = _memory(mem, mem_norm_g[layer], w_mem_kv[layer].astype(BF16), qc, q_norm_c[layer], k_norm_c[layer], seq)

        wr_rows = w_router[layer].astype(F32).T[r2e]
        wr_hi = wr_rows.astype(BF16)
        wr_lo = (wr_rows - wr_hi.astype(F32)).astype(BF16)
        x2, u_packed, s_t = _merge(
            xf, ya, yb, yc, gates,
            w_branch_a[layer].astype(BF16), w_branch_b[layer].astype(BF16), w_branch_c[layer].astype(BF16),
            w_out[layer].astype(BF16), ffn_norm_g[layer], wr_hi, wr_lo,
            w_shared_up[layer].astype(BF16), w_shared_down[layer].astype(BF16))

        idx_t, w_t, rank_t, cnt = _route(s_t, router_bias[layer][r2e], t)

        bm = BM_EXPERT
        counts = jnp.zeros((N_EXPERTS,), I32).at[r2e].set(cnt[:, 0].astype(I32))
        padded = (counts + bm - 1) // bm * bm
        pend = jnp.cumsum(padded)
        pstart = pend - padded
        n_rows = t * TOP_K + N_EXPERTS * bm
        nblk = n_rows // bm
        block_expert = jnp.minimum(
            jnp.searchsorted(pend, jnp.arange(nblk, dtype=I32) * bm, side='right'), N_EXPERTS - 1).astype(I32)
        n_used = (pend[-1:] // bm).astype(I32)
        zrow = jnp.where(padded > counts, pend - bm, -1).astype(I32)

        dest_tiles = _dest_rows(pstart.astype(I32), idx_t, rank_t, TS_ROWS)
        xs = _dispatch(zrow, dest_tiles, u_packed, n_rows)
        ys = _experts(block_expert, n_used, xs, w_exp_up[layer].astype(BF16), w_exp_down[layer].astype(BF16))
        xf = _combine(dest_tiles, ys, x2, w_t.T)
    return xf.reshape(batch, seq, d)
```

```python
import functools
import math

import numpy as np
import jax
import jax.numpy as jnp
from jax import lax
from jax.experimental import pallas as pl
from jax.experimental.pallas import tpu as pltpu

F32 = jnp.float32
BF16 = jnp.bfloat16
I32 = jnp.int32
U32 = jnp.uint32

HEAD_DIM = 64
EPS = 1e-6
A_HEADS = 6
A_PATTERNS = ((128, 1), (512, 4), (2048, 16))
BAND = 128
N_BUCKETS = 32
REL_MAX_DIST = 2048
B_HEADS = 4
B_KEY_DIM = 64
B_VAL_DIM = 96
B_CHUNK = 64
C_HEADS = 4
N_EXPERTS = 64
TOP_K = 8
N_GROUPS = 8
TOPK_GROUPS = 4
EXPERT_HIDDEN = 256
ROUTED_SCALE = 2.5

A_WIDTH = A_HEADS * HEAD_DIM
B_QK_WIDTH = B_HEADS * B_KEY_DIM
B_WIDTH = B_HEADS * B_VAL_DIM
C_WIDTH = C_HEADS * HEAD_DIM

NEG = -1e30
HI_MASK = 0xFFFF0000

LANES = 128
VMEM_LIMIT = 56 * 1024 * 1024

TM_PROJ = 512
TR_ROUTE = 512
TS_ROWS = 256
BM_EXPERT = 512


def _params(sem, vmem=VMEM_LIMIT):
    return pltpu.CompilerParams(dimension_semantics=sem, vmem_limit_bytes=vmem)


def _nt(a, b):
    return lax.dot_general(a, b, (((1,), (1,)), ((), ())), preferred_element_type=F32)


def _tn(a, b):
    return lax.dot_general(a, b, (((0,), (0,)), ((), ())), preferred_element_type=F32)


def _mm(a, b):
    return jnp.dot(a, b, preferred_element_type=F32)


def _split3(t):
    t1 = t.astype(BF16)
    r1 = t - t1.astype(F32)
    t2 = r1.astype(BF16)
    t3 = (r1 - t2.astype(F32)).astype(BF16)
    return t1, t2, t3


def _pack_pairs(y):
    n = y.shape[1] // 2
    bits = pltpu.bitcast(y.astype(BF16).astype(F32), U32)
    return (bits[:, :n] & U32(HI_MASK)) | (bits[:, n:] >> U32(16))


def _unpack_pairs(p):
    hi = pltpu.bitcast(p & U32(HI_MASK), F32)
    lo = pltpu.bitcast(p << U32(16), F32)
    return hi, lo


def _inproj_kernel(x_ref, g_ref, w_ref, qkv_ref, qf_ref, io_ref, qc_ref, gate_ref):
    x = x_ref[...]
    ms = jnp.mean(x * x, axis=-1, keepdims=True)
    h = (x * lax.rsqrt(ms + EPS) * g_ref[...]).astype(BF16)
    col = 0
    for out in (qkv_ref, qf_ref, io_ref, qc_ref, gate_ref):
        width = out.shape[1]
        for a in range(0, width, 512):
            b = min(a + 512, width)
            out[:, a:b] = _mm(h, w_ref[:, col + a:col + b]).astype(out.dtype)
        col += width


def _inproj(x2d, g, w_bf16):
    t, d = x2d.shape
    widths = (3 * A_WIDTH, 2 * B_QK_WIDTH, 2 * B_WIDTH, C_WIDTH, 3 * d)
    dtypes = (BF16, F32, F32, BF16, BF16)
    tm = TM_PROJ
    return pl.pallas_call(
        _inproj_kernel,
        grid=(t // tm,),
        in_specs=[
            pl.BlockSpec((tm, d), lambda i: (i, 0)),
            pl.BlockSpec((1, d), lambda i: (0, 0)),
            pl.BlockSpec(w_bf16.shape, lambda i: (0, 0), pipeline_mode=pl.Buffered(1)),
        ],
        out_specs=[pl.BlockSpec((tm, w), lambda i: (i, 0)) for w in widths],
        out_shape=[jax.ShapeDtypeStruct((t, w), dt) for w, dt in zip(widths, dtypes)],
        compiler_params=_params(("parallel",)),
        name="inproj",
    )(x2d, g.reshape(1, d), w_bf16)


def _bucket_tables():
    w = BAND
    i = np.arange(w)[:, None]
    c = np.arange(2 * w)[None, :]
    steps = w + i - c
    in_band = (steps >= 0) & (steps <= w)
    max_exact = N_BUCKETS // 2
    tabs = []
    for _, dil in A_PATTERNS:
        d = (np.clip(steps, 0, w) * dil).astype(np.int32)
        ratio = np.maximum(d, 1).astype(np.float32) / np.float32(max_exact)
        large = max_exact + (np.log(ratio) / np.float32(math.log(REL_MAX_DIST / max_exact))
                             * np.float32(N_BUCKETS - max_exact)).astype(np.int32)
        bucket = np.where(d < max_exact, d, np.minimum(large, N_BUCKETS - 1))
        tabs.append(np.where(in_band, bucket, -1))
    return np.stack(tabs).astype(np.int32)


def _bias_kernel(rb_ref, bucket_ref, out_ref):
    h = pl.program_id(1)
    bk = bucket_ref[0]
    acc = jnp.full(bk.shape, NEG, F32)
    for b in range(N_BUCKETS):
        acc = jnp.where(bk == b, rb_ref[b, h], acc)
    out_ref[0, 0] = acc


def _bias_tables(rel_bias):
    buckets = jnp.asarray(_bucket_tables())
    p = len(A_PATTERNS)
    return pl.pallas_call(
        _bias_kernel,
        grid=(p, A_HEADS),
        in_specs=[
            pl.BlockSpec(memory_space=pltpu.SMEM),
            pl.BlockSpec((1, BAND, 2 * BAND), lambda i, h: (i, 0, 0)),
        ],
        out_specs=pl.BlockSpec((1, 1, BAND, 2 * BAND), lambda i, h: (i, h, 0, 0)),
        out_shape=jax.ShapeDtypeStruct((p, A_HEADS, BAND, 2 * BAND), F32),
        compiler_params=_params(("arbitrary", "arbitrary")),
        name="attn_bias",
    )(rel_bias.astype(F32), buckets)


def _attn_kernel(q_ref, k_ref, v_ref, qg_ref, kg_ref, tab_ref, o_ref,
                 qn, kn, vf, qp, kp0, kp1, vp, mrun, lrun, arun):
    s = q_ref.shape[0]
    w = BAND
    lane = lax.broadcasted_iota(I32, (1, LANES), 1)
    h0 = lane < HEAD_DIM
    rows = 256

    def norm_body(c, carry):
        r0 = pl.multiple_of(c * rows, rows)
        for src, dst, g in ((q_ref, qn, qg_ref), (k_ref, kn, kg_ref)):
            t = src[pl.ds(r0, rows), :].astype(F32)
            sq = t * t
            s0 = jnp.sum(jnp.where(h0, sq, 0.0), axis=-1, keepdims=True)
            s1 = jnp.sum(jnp.where(h0, 0.0, sq), axis=-1, keepdims=True)
            ms = jnp.where(h0, s0, s1) * (1.0 / HEAD_DIM)
            dst[pl.ds(r0, rows), :] = t * lax.rsqrt(ms + EPS) * g[...]
        vf[pl.ds(r0, rows), :] = v_ref[pl.ds(r0, rows), :].astype(F32)
        return carry

    lax.fori_loop(0, s // rows, norm_body, 0)

    mrun[...] = jnp.full(mrun.shape, NEG, F32)
    lrun[...] = jnp.zeros(lrun.shape, F32)
    arun[...] = jnp.zeros(arun.shape, F32)
    zpad = jnp.zeros((w, LANES), BF16)
    kp0[0:w, :] = zpad
    kp1[0:w, :] = zpad
    vp[0:w, :] = zpad

    col = lax.broadcasted_iota(I32, (1, 2 * w), 1)

    for p, (window, dil) in enumerate(A_PATTERNS):
        seg = s // dil
        nb = seg // w
        nblk = s // w

        def src_index(start):
            if dil == 1:
                return pl.ds(start, w)
            return pl.ds(start, w, stride=dil)

        for r in range(dil):
            for c in range(nb):
                idx = src_index(r + dil * w * c)
                dst = w + r * seg + c * w
                kv = kn[idx, :]
                qp[dst:dst + w, :] = qn[idx, :].astype(BF16)
                kp0[dst:dst + w, :] = jnp.where(h0, kv, 0.0).astype(BF16)
                kp1[dst:dst + w, :] = jnp.where(h0, 0.0, kv).astype(BF16)
                vp[dst:dst + w, :] = vf[idx, :].astype(BF16)

        for i in range(nblk):
            row0 = i * w
            qb = qp[row0 + w:row0 + 2 * w, :]
            vw = vp[row0:row0 + 2 * w, :]
            n = i % nb
            r = i // nb
            res = []
            for h, kp in ((0, kp0), (1, kp1)):
                kw = kp[row0:row0 + 2 * w, :]
                lg = _nt(qb, kw) + tab_ref[p, h]
                if n == 0:
                    lg = jnp.where(col >= w, lg, NEG)
                m = jnp.max(lg, axis=-1, keepdims=True)
                pe = jnp.exp(lg - m)
                l = jnp.sum(pe, axis=-1, keepdims=True)
                pv = _mm(pe.astype(BF16), vw)
                res.append((m, l, pv))
            mc = jnp.where(h0, res[0][0], res[1][0])
            lc = jnp.where(h0, res[0][1], res[1][1])
            ac = jnp.where(h0, res[0][2], res[1][2])
            tok = src_index(r + dil * w * n)
            mo = mrun[tok, :]
            mn = jnp.maximum(mo, mc)
            ea = jnp.exp(mo - mn)
            eb = jnp.exp(mc - mn)
            lrun[tok, :] = ea * lrun[tok, :] + eb * lc
            arun[tok, :] = ea * arun[tok, :] + eb * ac
            mrun[tok, :] = mn

    def out_body(c, carry):
        r0 = pl.multiple_of(c * rows, rows)
        o_ref[pl.ds(r0, rows), :] = (arun[pl.ds(r0, rows), :] / lrun[pl.ds(r0, rows), :]).astype(o_ref.dtype)
        return carry

    lax.fori_loop(0, s // rows, out_body, 0)


def _attention(qkv, q_gain, k_gain, tab, batch, seq):
    t = qkv.shape[0]
    npair = A_HEADS // 2
    qg = (jnp.tile(q_gain.astype(F32), 2) * HEAD_DIM ** -0.5).reshape(1, LANES)
    kg = jnp.tile(k_gain.astype(F32), 2).reshape(1, LANES)
    spad = seq + BAND
    return pl.pallas_call(
        _attn_kernel,
        grid=(batch, npair),
        in_specs=[
            pl.BlockSpec((seq, LANES), lambda b, h: (b, h)),
            pl.BlockSpec((seq, LANES), lambda b, h: (b, npair + h)),
            pl.BlockSpec((seq, LANES), lambda b, h: (b, 2 * npair + h)),
            pl.BlockSpec((1, LANES), lambda b, h: (0, 0)),
            pl.BlockSpec((1, LANES), lambda b, h: (0, 0)),
            pl.BlockSpec((len(A_PATTERNS), 2, BAND, 2 * BAND), lambda b, h: (0, h, 0, 0)),
        ],
        out_specs=pl.BlockSpec((seq, LANES), lambda b, h: (b, h)),
        out_shape=jax.ShapeDtypeStruct((t, A_WIDTH), BF16),
        scratch_shapes=[
            pltpu.VMEM((seq, LANES), F32),
            pltpu.VMEM((seq, LANES), F32),
            pltpu.VMEM((seq, LANES), F32),
            pltpu.VMEM((spad, LANES), BF16),
            pltpu.VMEM((spad, LANES), BF16),
            pltpu.VMEM((spad, LANES), BF16),
            pltpu.VMEM((spad, LANES), BF16),
            pltpu.VMEM((seq, LANES), F32),
            pltpu.VMEM((seq, LANES), F32),
            pltpu.VMEM((seq, LANES), F32),
        ],
        compiler_params=_params(("parallel", "parallel")),
        name="dilated_attn",
    )(qkv, qkv, qkv, qg, kg, tab)


def _hgrn_consts():
    c = B_CHUNK
    hk = B_QK_WIDTH
    hv = B_WIDTH
    grp = 16
    tri = np.tril(np.ones((c, c), np.float32))
    gt = np.zeros((B_HEADS * grp, grp * hk), np.float32)
    for h in range(B_HEADS):
        for j in range(grp):
            gt[h * grp + j, j * hk + h * B_KEY_DIM:j * hk + (h + 1) * B_KEY_DIM] = 1.0
    bd = np.zeros((hv, hk), np.float32)
    vm = np.zeros((B_HEADS * grp, hv), np.float32)
    hm = np.zeros((hv, hv), np.float32)
    for h in range(B_HEADS):
        bd[h * B_VAL_DIM:(h + 1) * B_VAL_DIM, h * B_KEY_DIM:(h + 1) * B_KEY_DIM] = 1.0
        vm[h * grp:(h + 1) * grp, h * B_VAL_DIM:(h + 1) * B_VAL_DIM] = 1.0
        hm[h * B_VAL_DIM:(h + 1) * B_VAL_DIM, h * B_VAL_DIM:(h + 1) * B_VAL_DIM] = 1.0
    return tri, gt, bd, vm, hm, grp


def _hgrn_kernel(qb_ref, fb_ref, ib_ref, og_ref, lbl_ref, onorm_ref, tri_ref, gt_ref, bd_ref, vm_ref, hm_ref,
                 o_ref, st, dstack, bsc, qsc, ksc, *, grp, layer):
    s = qb_ref.shape[0]
    c = B_CHUNK
    hk = B_QK_WIDTH
    ngrp = c // grp

    st[...] = jnp.zeros(st.shape, F32)
    dstack[...] = jnp.zeros(dstack.shape, BF16)
    lbl = lbl_ref[...]
    e = jnp.exp(lbl - jnp.max(lbl, axis=0, keepdims=True))
    lb = jnp.sum(e[0:layer + 1, :], axis=0, keepdims=True) / jnp.sum(e, axis=0, keepdims=True)
    tri = tri_ref[...]

    def chunk(ci, carry):
        c0 = pl.multiple_of(ci * c, c)
        qr = qb_ref[pl.ds(c0, c), :]
        q = qr * jax.nn.sigmoid(qr) * (B_KEY_DIM ** -0.5)
        f = lb + (1.0 - lb) * jax.nn.sigmoid(fb_ref[pl.ds(c0, c), :])
        g1, g2, g3 = _split3(jnp.log(f))
        bcum = _mm(tri, g1) + _mm(tri, g2) + _mm(tri, g3)
        kk = 1.0 - f
        v = ib_ref[pl.ds(c0, c), :]
        vb = v.astype(BF16)

        stv = st[...]
        o = _nt((q * jnp.exp(bcum)).astype(BF16), stv.astype(BF16))
        blast = bcum[c - 1:c, :]
        kd = kk * jnp.exp(blast - bcum)
        st[...] = stv * jnp.exp(blast) + _tn(vb, kd.astype(BF16)) * bd_ref[...]

        bsc[...] = bcum
        qsc[...] = q
        ksc[...] = kk
        for sp in range(c):
            t0 = (sp // 16) * 16
            n = c - t0
            dif = bsc[t0:c, :] - bsc[sp:sp + 1, :]
            trow = lax.broadcasted_iota(I32, (n, 1), 0) + t0
            ex = jnp.exp(jnp.where(trow >= sp, dif, NEG))
            dv = (qsc[t0:c, :] * ksc[sp:sp + 1, :]) * ex
            dstack[t0:c, sp * hk:(sp + 1) * hk] = dv.astype(BF16)

        for gi in range(ngrp):
            sc = _nt(dstack[:, gi * grp * hk:(gi + 1) * grp * hk], gt_ref[...])
            vg = v[gi * grp:(gi + 1) * grp, :]
            vg = (jnp.concatenate([vg] * B_HEADS, axis=0) * vm_ref[...]).astype(BF16)
            o = o + _mm(sc.astype(BF16), vg)

        o2 = o * o
        o2h = o2.astype(BF16)
        o2l = (o2 - o2h.astype(F32)).astype(BF16)
        ms = (_mm(o2h, hm_ref[...]) + _mm(o2l, hm_ref[...])) * (1.0 / B_VAL_DIM)
        ogr = og_ref[pl.ds(c0, c), :]
        y = o * lax.rsqrt(ms + EPS) * onorm_ref[...] * (ogr * jax.nn.sigmoid(ogr))
        o_ref[pl.ds(c0, c), :] = y.astype(o_ref.dtype)
        return carry

    lax.fori_loop(0, s // c, chunk, 0)


def _hgrn(qf, io, lb_logits, layer, out_norm, batch, seq):
    t = qf.shape[0]
    tri, gt, bd, vm, hm, grp = _hgrn_consts()
    nslot = lb_logits.shape[0]
    const = lambda a, dt: jnp.asarray(a, dt)
    full = lambda shp: pl.BlockSpec(shp, lambda b: (0,) * len(shp))
    return pl.pallas_call(
        functools.partial(_hgrn_kernel, grp=grp, layer=layer),
        grid=(batch,),
        in_specs=[
            pl.BlockSpec((seq, B_QK_WIDTH), lambda b: (b, 0)),
            pl.BlockSpec((seq, B_QK_WIDTH), lambda b: (b, 1)),
            pl.BlockSpec((seq, B_WIDTH), lambda b: (b, 0)),
            pl.BlockSpec((seq, B_WIDTH), lambda b: (b, 1)),
            full((nslot, B_QK_WIDTH)),
            full((1, B_WIDTH)),
            full(tri.shape), full(gt.shape), full(bd.shape), full(vm.shape), full(hm.shape),
        ],
        out_specs=pl.BlockSpec((seq, B_WIDTH), lambda b: (b, 0)),
        out_shape=jax.ShapeDtypeStruct((t, B_WIDTH), BF16),
        scratch_shapes=[
            pltpu.VMEM((B_WIDTH, B_QK_WIDTH), F32),
            pltpu.VMEM((B_CHUNK, B_CHUNK * B_QK_WIDTH), BF16),
            pltpu.VMEM((B_CHUNK, B_QK_WIDTH), F32),
            pltpu.VMEM((B_CHUNK, B_QK_WIDTH), F32),
            pltpu.VMEM((B_CHUNK, B_QK_WIDTH), F32),
        ],
        compiler_params=_params(("parallel",)),
        name="hgrn2",
    )(qf, qf, io, io, lb_logits.astype(F32), out_norm.astype(F32).reshape(1, B_WIDTH),
      const(tri, BF16), const(gt, BF16), const(bd, F32), const(vm, F32), const(hm, BF16))


def _head_ms(t, masks):
    sq = t * t
    ms = jnp.zeros_like(t)
    for mk in masks:
        ms = jnp.where(mk, jnp.sum(jnp.where(mk, sq, 0.0), axis=-1, keepdims=True), ms)
    return ms * (1.0 / HEAD_DIM)


def _mem_kernel(mem_ref, mg_ref, wkv_ref, q_ref, qg_ref, kg_ref, o_ref, ksc, vsc):
    s = q_ref.shape[0]
    lane = lax.broadcasted_iota(I32, (1, C_WIDTH), 1)
    masks = [(lane >= h * HEAD_DIM) & (lane < (h + 1) * HEAD_DIM) for h in range(C_HEADS)]

    m = mem_ref[0]
    ms = jnp.mean(m * m, axis=-1, keepdims=True)
    mn = (m * lax.rsqrt(ms + EPS) * mg_ref[...]).astype(BF16)
    kv = _mm(mn, wkv_ref[...])
    km = kv[:, :C_WIDTH]
    k = km * lax.rsqrt(_head_ms(km, masks) + EPS) * kg_ref[...]
    for h in range(C_HEADS):
        ksc[h] = jnp.where(masks[h], k, 0.0).astype(BF16)
    vsc[...] = kv[:, C_WIDTH:].astype(BF16)

    rows = 256

    def body(c, carry):
        r0 = pl.multiple_of(c * rows, rows)
        q = q_ref[pl.ds(r0, rows), :].astype(F32)
        qn = (q * lax.rsqrt(_head_ms(q, masks) + EPS) * qg_ref[...]).astype(BF16)
        out = jnp.zeros((rows, C_WIDTH), F32)
        for h in range(C_HEADS):
            lg = _nt(qn, ksc[h])
            mx = jnp.max(lg, axis=-1, keepdims=True)
            pe = jnp.exp(lg - mx)
            l = jnp.sum(pe, axis=-1, keepdims=True)
            out = jnp.where(masks[h], _mm(pe.astype(BF16), vsc[...]) / l, out)
        o_ref[pl.ds(r0, rows), :] = out.astype(o_ref.dtype)
        return carry

    lax.fori_loop(0, s // rows, body, 0)


def _memory(mem, mem_g, wkv_bf16, qc, q_gain, k_gain, seq):
    batch, mlen, d = mem.shape
    t = qc.shape[0]
    qg = (jnp.tile(q_gain.astype(F32), C_HEADS) * HEAD_DIM ** -0.5).reshape(1, C_WIDTH)
    kg = jnp.tile(k_gain.astype(F32), C_HEADS).reshape(1, C_WIDTH)
    return pl.pallas_call(
        _mem_kernel,
        grid=(batch,),
        in_specs=[
            pl.BlockSpec((1, mlen, d), lambda b: (b, 0, 0)),
            pl.BlockSpec((1, d), lambda b: (0, 0)),
            pl.BlockSpec(wkv_bf16.shape, lambda b: (0, 0)),
            pl.BlockSpec((seq, C_WIDTH), lambda b: (b, 0)),
            pl.BlockSpec((1, C_WIDTH), lambda b: (0, 0)),
            pl.BlockSpec((1, C_WIDTH), lambda b: (0, 0)),
        ],
        out_specs=pl.BlockSpec((seq, C_WIDTH), lambda b: (b, 0)),
        out_shape=jax.ShapeDtypeStruct((t, C_WIDTH), BF16),
        scratch_shapes=[
            pltpu.VMEM((C_HEADS, mlen, C_WIDTH), BF16),
            pltpu.VMEM((mlen, C_WIDTH), BF16),
        ],
        compiler_params=_params(("parallel",)),
        name="mem_xattn",
    )(mem, mem_g.astype(F32).reshape(1, d), wkv_bf16, qc, qg, kg)


def _merge_kernel(x_ref, ya_ref, yb_ref, yc_ref, gate_ref, wa_ref, wb_ref, wc_ref, wo_ref, fg_ref,
                  wrh_ref, wrl_ref, wsu_ref, wsd_ref, x2_ref, up_ref, st_ref):
    d = x_ref.shape[1]
    mixed = jnp.zeros(x_ref.shape, F32)
    for i, (y_ref, w_ref) in enumerate(((ya_ref, wa_ref), (yb_ref, wb_ref), (yc_ref, wc_ref))):
        g = jax.nn.sigmoid(gate_ref[:, i * d:(i + 1) * d].astype(F32))
        mixed = mixed + g * _mm(y_ref[...], w_ref[...])
    x1 = x_ref[...] + _mm(mixed.astype(BF16), wo_ref[...])
    ms = jnp.mean(x1 * x1, axis=-1, keepdims=True)
    u = x1 * lax.rsqrt(ms + EPS) * fg_ref[...]
    ub = u.astype(BF16)
    ul = (u - ub.astype(F32)).astype(BF16)
    logits = _nt(wrh_ref[...], ub) + _nt(wrl_ref[...], ub) + _nt(wrh_ref[...], ul)
    st_ref[...] = jax.nn.sigmoid(logits)
    hid = _mm(ub, wsu_ref[...])
    nh = hid.shape[1] // 2
    act = (jax.nn.silu(hid[:, :nh]) * hid[:, nh:]).astype(BF16)
    x2_ref[...] = x1 + _mm(act, wsd_ref[...])
    up_ref[...] = _pack_pairs(ub.astype(F32))


def _merge(x2d, ya, yb, yc, gates, wa, wb, wc, wo, fg, wr_hi, wr_lo, wsu, wsd):
    t, d = x2d.shape
    tm = TM_PROJ
    row = lambda w: pl.BlockSpec((tm, w), lambda i: (i, 0))
    full = lambda a: pl.BlockSpec(a.shape, lambda i: (0,) * a.ndim)
    fg2 = fg.astype(F32).reshape(1, d)
    return pl.pallas_call(
        _merge_kernel,
        grid=(t // tm,),
        in_specs=[row(d), row(A_WIDTH), row(B_WIDTH), row(C_WIDTH), row(3 * d),
                  full(wa), full(wb), full(wc), full(wo), full(fg2), full(wr_hi), full(wr_lo),
                  full(wsu), full(wsd)],
        out_specs=[row(d), row(d // 2), pl.BlockSpec((N_EXPERTS, tm), lambda i: (0, i))],
        out_shape=[jax.ShapeDtypeStruct((t, d), F32),
                   jax.ShapeDtypeStruct((t, d // 2), U32),
                   jax.ShapeDtypeStruct((N_EXPERTS, t), F32)],
        compiler_params=_params(("parallel",)),
        name="merge_router_shared",
    )(x2d, ya, yb, yc, gates, wa, wb, wc, wo, fg2, wr_hi, wr_lo, wsu, wsd)


def _row_to_expert():
    per_group = N_EXPERTS // N_GROUPS
    rows = np.arange(N_EXPERTS)
    return (rows % N_GROUPS) * per_group + rows // N_GROUPS


def _route_kernel(s_ref, bias_ref, eidx_ref, ut_ref, ones_ref, idx_ref, w_ref, rank_ref, cnt_ref, carry):
    tr = s_ref.shape[1]
    per_group = N_EXPERTS // N_GROUPS
    ng = N_GROUPS
    nrep = tr // LANES
    ninf = -jnp.inf

    @pl.when(pl.program_id(0) == 0)
    def _():
        carry[...] = jnp.zeros(carry.shape, F32)

    s_all = s_ref[...]
    bias = jnp.concatenate([bias_ref[...]] * nrep, axis=1)
    eidx = jnp.concatenate([eidx_ref[...]] * nrep, axis=1)
    sv = [s_all[i * ng:(i + 1) * ng, :] for i in range(per_group)]
    ev = [eidx[i * ng:(i + 1) * ng, :] for i in range(per_group)]
    cv = [sv[i] + bias[i * ng:(i + 1) * ng, :] for i in range(per_group)]

    def vmax(xs):
        out = xs[0]
        for x in xs[1:]:
            out = jnp.maximum(out, x)
        return out

    def vmin(xs):
        out = xs[0]
        for x in xs[1:]:
            out = jnp.minimum(out, x)
        return out

    m1 = vmax(cv)
    i1 = vmin([jnp.where(cv[i] == m1, i, per_group) for i in range(per_group)])
    m2 = vmax([jnp.where(i1 == i, ninf, cv[i]) for i in range(per_group)])
    gs = m1 + m2
    giota = lax.broadcasted_iota(I32, (ng, tr), 0)
    gmask = jnp.zeros((ng, tr), jnp.bool_)
    for _ in range(TOPK_GROUPS):
        mx = jnp.max(gs, axis=0, keepdims=True)
        pick = jnp.min(jnp.where(gs == mx, giota, ng), axis=0, keepdims=True)
        sel = giota == pick
        gmask = jnp.logical_or(gmask, sel)
        gs = jnp.where(sel, ninf, gs)

    cm = [jnp.where(gmask, cv[i], ninf) for i in range(per_group)]
    idx_out = jnp.zeros((TOP_K, tr), I32)
    w_out = jnp.zeros((TOP_K, tr), F32)
    jiota = lax.broadcasted_iota(I32, (TOP_K, tr), 0)
    for j in range(TOP_K):
        mx = jnp.max(vmax(cm), axis=0, keepdims=True)
        emin = jnp.min(vmin([jnp.where(cm[i] == mx, ev[i], N_EXPERTS) for i in range(per_group)]),
                       axis=0, keepdims=True)
        hit = [ev[i] == emin for i in range(per_group)]
        ssel = hit_sum = None
        for i in range(per_group):
            term = jnp.where(hit[i], sv[i], 0.0)
            hit_sum = term if hit_sum is None else hit_sum + term
        ssel = jnp.sum(hit_sum, axis=0, keepdims=True)
        cm = [jnp.where(hit[i], ninf, cm[i]) for i in range(per_group)]
        idx_out = jnp.where(jiota == j, emin, idx_out)
        w_out = jnp.where(jiota == j, ssel, w_out)
    idx_ref[...] = idx_out
    w_ref[...] = ROUTED_SCALE * w_out / jnp.sum(w_out, axis=0, keepdims=True)

    chosen = [jnp.where(jnp.logical_and(gmask, cm[i] == ninf), 1.0, 0.0) for i in range(per_group)]
    mh = jnp.concatenate(chosen, axis=0).astype(BF16)
    cr = carry[...]
    rank_full = _mm(mh, ut_ref[...]) + jnp.concatenate([cr] * nrep, axis=1)
    rv = [rank_full[i * ng:(i + 1) * ng, :] for i in range(per_group)]
    rank_out = jnp.zeros((TOP_K, tr), F32)
    for j in range(TOP_K):
        ej = idx_out[j:j + 1, :]
        acc = None
        for i in range(per_group):
            term = jnp.where(ev[i] == ej, rv[i], 0.0)
            acc = term if acc is None else acc + term
        rank_out = jnp.where(jiota == j, jnp.sum(acc, axis=0, keepdims=True), rank_out)
    rank_ref[...] = rank_out.astype(I32)
    carry[...] = cr + _mm(mh, ones_ref[...])
    cnt_ref[...] = carry[...]


def _route(s_t, bias_rows, t):
    tr = TR_ROUTE
    r2e = _row_to_expert()
    eidx = jnp.asarray(np.broadcast_to(r2e[:, None], (N_EXPERTS, LANES)).astype(np.int32))
    bias = jnp.broadcast_to(bias_rows.astype(F32)[:, None], (N_EXPERTS, LANES))
    ut = jnp.asarray(np.triu(np.ones((tr, tr), np.float32), k=1), BF16)
    ones = jnp.ones((tr, LANES), BF16)
    full = lambda a: pl.BlockSpec(a.shape, lambda i: (0,) * a.ndim)
    tile = pl.BlockSpec((TOP_K, tr), lambda i: (0, i))
    return pl.pallas_call(
        _route_kernel,
        grid=(t // tr,),
        in_specs=[pl.BlockSpec((N_EXPERTS, tr), lambda i: (0, i)), full(bias), full(eidx), full(ut), full(ones)],
        out_specs=[tile, tile, tile, pl.BlockSpec((N_EXPERTS, LANES), lambda i: (0, 0))],
        out_shape=[jax.ShapeDtypeStruct((TOP_K, t), I32),
                   jax.ShapeDtypeStruct((TOP_K, t), F32),
                   jax.ShapeDtypeStruct((TOP_K, t), I32),
                   jax.ShapeDtypeStruct((N_EXPERTS, LANES), F32)],
        scratch_shapes=[pltpu.VMEM((N_EXPERTS, LANES), F32)],
        compiler_params=_params(("arbitrary",)),
        name="route_topk_rank",
    )(s_t, bias, eidx, ut, ones)


def _dest_kernel(ps_ref, idx_ref, rank_ref, dest_ref):
    idx = idx_ref[...]
    acc = rank_ref[...]
    for e in range(N_EXPERTS):
        acc = acc + jnp.where(idx == e, ps_ref[e], 0)
    dest_ref[0] = acc


def _dest_rows(pstart, idx_t, rank_t, ts):
    t = idx_t.shape[1]
    tile = pl.BlockSpec((TOP_K, ts), lambda i, ps: (0, i))
    return pl.pallas_call(
        _dest_kernel,
        grid_spec=pltpu.PrefetchScalarGridSpec(
            num_scalar_prefetch=1,
            grid=(t // ts,),
            in_specs=[tile, tile],
            out_specs=pl.BlockSpec((1, TOP_K, ts), lambda i, ps: (i, 0, 0)),
        ),
        out_shape=jax.ShapeDtypeStruct((t // ts, TOP_K, ts), I32),
        compiler_params=_params(("arbitrary",)),
        name="dest_rows",
    )(pstart, idx_t, rank_t)


def _dispatch_kernel(zrow_ref, dest_hbm, u_ref, xs_hbm, dsm, zbuf, sem, zsem, isem):
    i = pl.program_id(0)
    nt = pl.num_programs(0)
    ts = u_ref.shape[0]
    bm = zbuf.shape[0]

    def idx_copy(tile, slot):
        return pltpu.make_async_copy(dest_hbm.at[tile], dsm.at[slot], isem.at[slot])

    def zero_copy(e):
        return pltpu.make_async_copy(zbuf, xs_hbm.at[pl.ds(pl.multiple_of(zrow_ref[e], bm), bm)], zsem)

    @pl.when(i == 0)
    def _():
        idx_copy(0, 0).start()
        zbuf[...] = jnp.zeros(zbuf.shape, zbuf.dtype)

        def zstart(e, carry):
            @pl.when(zrow_ref[e] >= 0)
            def _():
                zero_copy(e).start()
            return carry

        def zwait(e, carry):
            @pl.when(zrow_ref[e] >= 0)
            def _():
                zero_copy(e).wait()
            return carry

        lax.fori_loop(0, N_EXPERTS, zstart, 0)
        lax.fori_loop(0, N_EXPERTS, zwait, 0)

    slot = lax.rem(i, 2)
    idx_copy(i, slot).wait()

    @pl.when(i + 1 < nt)
    def _():
        idx_copy(i + 1, 1 - slot).start()

    def row_copy(t, j):
        return pltpu.make_async_copy(u_ref.at[t], xs_hbm.at[dsm[slot, j, t]], sem)

    def issue(t, carry):
        for j in range(TOP_K):
            row_copy(t, j).start()
        return carry

    def drain(t, carry):
        for j in range(TOP_K):
            row_copy(t, j).wait()
        return carry

    lax.fori_loop(0, ts, issue, 0)
    lax.fori_loop(0, ts, drain, 0)


def _dispatch(zrow, dest_tiles, u_packed, n_rows):
    t, half = u_packed.shape
    ts = TS_ROWS
    return pl.pallas_call(
        _dispatch_kernel,
        grid_spec=pltpu.PrefetchScalarGridSpec(
            num_scalar_prefetch=1,
            grid=(t // ts,),
            in_specs=[pl.BlockSpec(memory_space=pl.ANY),
                      pl.BlockSpec((ts, half), lambda i, z: (i, 0))],
            out_specs=pl.BlockSpec(memory_space=pl.ANY),
            scratch_shapes=[
                pltpu.SMEM((2, TOP_K, ts), I32),
                pltpu.VMEM((BM_EXPERT, half), U32),
                pltpu.SemaphoreType.DMA,
                pltpu.SemaphoreType.DMA,
                pltpu.SemaphoreType.DMA((2,)),
            ],
        ),
        out_shape=jax.ShapeDtypeStruct((n_rows, half), U32),
        compiler_params=_params(("arbitrary",)),
        name="dispatch_rows",
    )(zrow, dest_tiles, u_packed)


def _expert_kernel(be_ref, nu_ref, xs_ref, wu_ref, wd_ref, ys_ref):
    @pl.when(pl.program_id(0) < nu_ref[0])
    def _():
        xa, xb = _unpack_pairs(xs_ref[...])
        half = xa.shape[1]
        hid = _mm(xa.astype(BF16), wu_ref[0, :half, :]) + _mm(xb.astype(BF16), wu_ref[0, half:, :])
        nh = hid.shape[1] // 2
        act = (jax.nn.silu(hid[:, :nh]) * hid[:, nh:]).astype(BF16)
        ys_ref[...] = _pack_pairs(_mm(act, wd_ref[0]))


def _experts(block_expert, n_used, xs, wu, wd):
    n_rows, half = xs.shape
    bm = BM_EXPERT
    nblk = n_rows // bm
    d = 2 * half
    last = lambda b, nu: jnp.minimum(b, nu[0] - 1)
    return pl.pallas_call(
        _expert_kernel,
        grid_spec=pltpu.PrefetchScalarGridSpec(
            num_scalar_prefetch=2,
            grid=(nblk,),
            in_specs=[
                pl.BlockSpec((bm, half), lambda b, be, nu: (last(b, nu), 0)),
                pl.BlockSpec((1, d, wu.shape[2]), lambda b, be, nu: (be[last(b, nu)], 0, 0)),
                pl.BlockSpec((1, wd.shape[1], d), lambda b, be, nu: (be[last(b, nu)], 0, 0)),
            ],
            out_specs=pl.BlockSpec((bm, half), lambda b, be, nu: (last(b, nu), 0)),
        ),
        out_shape=jax.ShapeDtypeStruct((n_rows, half), U32),
        compiler_params=_params(("arbitrary",)),
        name="expert_swiglu",
    )(block_expert, n_used, xs, wu, wd)


def _combine_kernel(dest_hbm, ys_hbm, x2_ref, w_ref, o_ref, dsm, buf, sems, isem):
    i = pl.program_id(0)
    nt = pl.num_programs(0)
    ts = x2_ref.shape[0]
    half = buf.shape[3]

    def idx_copy(tile, slot):
        return pltpu.make_async_copy(dest_hbm.at[tile], dsm.at[slot], isem.at[slot])

    def row_copy(slot, t, j):
        return pltpu.make_async_copy(ys_hbm.at[dsm[slot, j, t]], buf.at[slot, j, t], sems.at[slot])

    def issue(slot):
        def body(t, carry):
            for j in range(TOP_K):
                row_copy(slot, t, j).start()
            return carry
        lax.fori_loop(0, ts, body, 0)

    slot = lax.rem(i, 2)

    @pl.when(i == 0)
    def _():
        idx_copy(0, 0).start()
        idx_copy(0, 0).wait()
        issue(0)

    @pl.when(i + 1 < nt)
    def _():
        idx_copy(i + 1, 1 - slot).start()
        idx_copy(i + 1, 1 - slot).wait()
        issue(1 - slot)

    def drain(t, carry):
        for j in range(TOP_K):
            row_copy(slot, t, j).wait()
        return carry

    lax.fori_loop(0, ts, drain, 0)

    acc_a = x2_ref[:, :half]
    acc_b = x2_ref[:, half:]
    for j in range(TOP_K):
        ya, yb = _unpack_pairs(buf[slot, j])
        wj = w_ref[:, j:j + 1]
        acc_a = acc_a + wj * ya
        acc_b = acc_b + wj * yb
    o_ref[:, :half] = acc_a
    o_ref[:, half:] = acc_b


def _combine(dest_tiles, ys, x2, w_tok):
    t, d = x2.shape
    ts = TS_ROWS
    half = d // 2
    return pl.pallas_call(
        _combine_kernel,
        grid=(t // ts,),
        in_specs=[pl.BlockSpec(memory_space=pl.ANY),
                  pl.BlockSpec(memory_space=pl.ANY),
                  pl.BlockSpec((ts, d), lambda i: (i, 0)),
                  pl.BlockSpec((ts, TOP_K), lambda i: (i, 0))],
        out_specs=pl.BlockSpec((ts, d), lambda i: (i, 0)),
        out_shape=jax.ShapeDtypeStruct((t, d), F32),
        scratch_shapes=[
            pltpu.SMEM((2, TOP_K, ts), I32),
            pltpu.VMEM((2, TOP_K, ts, half), U32),
            pltpu.SemaphoreType.DMA((2,)),
            pltpu.SemaphoreType.DMA((2,)),
        ],
        compiler_params=_params(("arbitrary",)),
        name="combine_rows",
    )(dest_tiles, ys, x2, w_tok)


def kernel(x, mem, attn_norm_g, w_in, q_norm_a, k_norm_a, rel_bias, lb_logits, out_norm_b, mem_norm_g, w_mem_kv,
           q_norm_c, k_norm_c, w_branch_a, w_branch_b, w_branch_c, w_out, ffn_norm_g, w_router, router_bias,
           w_exp_up, w_exp_down, w_shared_up, w_shared_down):
    batch, seq, d = x.shape
    t = batch * seq
    depth = w_in.shape[0]
    tab = _bias_tables(rel_bias)
    r2e = _row_to_expert()
    xf = x.reshape(t, d)
    for layer in range(depth):
        qkv, qf, io, qc, gates = _inproj(xf, attn_norm_g[layer], w_in[layer].astype(BF16))
        ya = _attention(qkv, q_norm_a[layer], k_norm_a[layer], tab, batch, seq)
        yb = _hgrn(qf, io, lb_logits, layer, out_norm_b[layer], batch, seq)
        yc = _memory(mem, mem_norm_g[layer], w_mem_kv[layer].astype(BF16), qc, q_norm_c[layer], k_norm_c[layer], seq)

        wr_rows = w_router[layer].astype(F32).T[r2e]
        wr_hi = wr_rows.astype(BF16)
        wr_lo = (wr_rows - wr_hi.astype(F32)).astype(BF16)
        x2, u_packed, s_t = _merge(
            xf, ya, yb, yc, gates,
            w_branch_a[layer].astype(BF16), w_branch_b[layer].astype(BF16), w_branch_c[layer].astype(BF16),
            w_out[layer].astype(BF16), ffn_norm_g[layer], wr_hi, wr_lo,
            w_shared_up[layer].astype(BF16), w_shared_down[layer].astype(BF16))

        idx_t, w_t, rank_t, cnt = _route(s_t, router_bias[layer][r2e], t)

        bm = BM_EXPERT
        counts = jnp.zeros((N_EXPERTS,), I32).at[r2e].set(cnt[:, 0].astype(I32))
        padded = (counts + bm - 1) // bm * bm
        pend = jnp.cumsum(padded)
        pstart = pend - padded
        n_rows = t * TOP_K + N_EXPERTS * bm
        nblk = n_rows // bm
        block_row = jnp.arange(nblk, dtype=I32) * bm
        block_expert = jnp.minimum(
            jnp.sum((pend[None, :] <= block_row[:, None]).astype(I32), axis=1), N_EXPERTS - 1).astype(I32)
        n_used = (pend[-1:] // bm).astype(I32)
        zrow = jnp.where(padded > counts, pend - bm, -1).astype(I32)

        dest_tiles = _dest_rows(pstart.astype(I32), idx_t, rank_t, TS_ROWS)
        xs = _dispatch(zrow, dest_tiles, u_packed, n_rows)
        ys = _experts(block_expert, n_used, xs, w_exp_up[layer].astype(BF16), w_exp_down[layer].astype(BF16))
        xf = _combine(dest_tiles, ys, x2, w_t.T)
    return xf.reshape(batch, seq, d)
```

```python
import functools
import math

import numpy as np
import jax
import jax.numpy as jnp
from jax import lax
from jax.experimental import pallas as pl
from jax.experimental.pallas import tpu as pltpu

F32 = jnp.float32
BF16 = jnp.bfloat16
I32 = jnp.int32
U32 = jnp.uint32

HEAD_DIM = 64
EPS = 1e-6
A_HEADS = 6
A_PATTERNS = ((128, 1), (512, 4), (2048, 16))
BAND = 128
N_BUCKETS = 32
REL_MAX_DIST = 2048
B_HEADS = 4
B_KEY_DIM = 64
B_VAL_DIM = 96
B_CHUNK = 64
C_HEADS = 4
N_EXPERTS = 64
TOP_K = 8
N_GROUPS = 8
TOPK_GROUPS = 4
EXPERT_HIDDEN = 256
ROUTED_SCALE = 2.5

A_WIDTH = A_HEADS * HEAD_DIM
B_QK_WIDTH = B_HEADS * B_KEY_DIM
B_WIDTH = B_HEADS * B_VAL_DIM
C_WIDTH = C_HEADS * HEAD_DIM

NEG = -1e30
HI_MASK = 0xFFFF0000

LANES = 128
SUBLANES = 8
VMEM_LIMIT = 56 * 1024 * 1024

TM_PROJ = 512
TB_MOE = 256
BM_EXPERT = 512
RC_MOE = 256
RL_MOE = -(-(TB_MOE * TOP_K + N_EXPERTS * (SUBLANES - 1)) // RC_MOE) * RC_MOE
PIECES_MOE = RL_MOE // SUBLANES
USED_BITS = (256, 128, 64, 32, 16, 8, 4, 2, 1)


def _params(sem, vmem=VMEM_LIMIT):
    return pltpu.CompilerParams(dimension_semantics=sem, vmem_limit_bytes=vmem)


def _nt(a, b):
    return lax.dot_general(a, b, (((1,), (1,)), ((), ())), preferred_element_type=F32)


def _tn(a, b):
    return lax.dot_general(a, b, (((0,), (0,)), ((), ())), preferred_element_type=F32)


def _mm(a, b):
    return jnp.dot(a, b, preferred_element_type=F32)


def _split3(t):
    t1 = t.astype(BF16)
    r1 = t - t1.astype(F32)
    t2 = r1.astype(BF16)
    t3 = (r1 - t2.astype(F32)).astype(BF16)
    return t1, t2, t3


def _pack_pairs(y):
    n = y.shape[1] // 2
    bits = pltpu.bitcast(y.astype(BF16).astype(F32), U32)
    return (bits[:, :n] & U32(HI_MASK)) | (bits[:, n:] >> U32(16))


def _unpack_pairs(p):
    hi = pltpu.bitcast(p & U32(HI_MASK), F32)
    lo = pltpu.bitcast(p << U32(16), F32)
    return hi, lo


def _inproj_kernel(x_ref, g_ref, w_ref, qkv_ref, qf_ref, io_ref, qc_ref, gate_ref):
    x = x_ref[...]
    ms = jnp.mean(x * x, axis=-1, keepdims=True)
    h = (x * lax.rsqrt(ms + EPS) * g_ref[...]).astype(BF16)
    col = 0
    for out in (qkv_ref, qf_ref, io_ref, qc_ref, gate_ref):
        width = out.shape[1]
        for a in range(0, width, 512):
            b = min(a + 512, width)
            out[:, a:b] = _mm(h, w_ref[:, col + a:col + b]).astype(out.dtype)
        col += width


def _inproj(x2d, g, w_bf16):
    t, d = x2d.shape
    widths = (3 * A_WIDTH, 2 * B_QK_WIDTH, 2 * B_WIDTH, C_WIDTH, 3 * d)
    dtypes = (BF16, F32, F32, BF16, BF16)
    tm = TM_PROJ
    return pl.pallas_call(
        _inproj_kernel,
        grid=(t // tm,),
        in_specs=[
            pl.BlockSpec((tm, d), lambda i: (i, 0)),
            pl.BlockSpec((1, d), lambda i: (0, 0)),
            pl.BlockSpec(w_bf16.shape, lambda i: (0, 0), pipeline_mode=pl.Buffered(1)),
        ],
        out_specs=[pl.BlockSpec((tm, w), lambda i: (i, 0)) for w in widths],
        out_shape=[jax.ShapeDtypeStruct((t, w), dt) for w, dt in zip(widths, dtypes)],
        compiler_params=_params(("parallel",)),
        name="inproj",
    )(x2d, g.reshape(1, d), w_bf16)


def _bucket_tables():
    w = BAND
    i = np.arange(w)[:, None]
    c = np.arange(2 * w)[None, :]
    steps = w + i - c
    in_band = (steps >= 0) & (steps <= w)
    max_exact = N_BUCKETS // 2
    tabs = []
    for _, dil in A_PATTERNS:
        d = (np.clip(steps, 0, w) * dil).astype(np.int32)
        ratio = np.maximum(d, 1).astype(np.float32) / np.float32(max_exact)
        large = max_exact + (np.log(ratio) / np.float32(math.log(REL_MAX_DIST / max_exact))
                             * np.float32(N_BUCKETS - max_exact)).astype(np.int32)
        bucket = np.where(d < max_exact, d, np.minimum(large, N_BUCKETS - 1))
        tabs.append(np.where(in_band, bucket, -1))
    return np.stack(tabs).astype(np.int32)


def _bias_kernel(rb_ref, bucket_ref, out_ref):
    h = pl.program_id(1)
    bk = bucket_ref[0]
    acc = jnp.full(bk.shape, NEG, F32)
    for b in range(N_BUCKETS):
        acc = jnp.where(bk == b, rb_ref[b, h], acc)
    out_ref[0, 0] = acc


def _bias_tables(rel_bias):
    buckets = jnp.asarray(_bucket_tables())
    p = len(A_PATTERNS)
    return pl.pallas_call(
        _bias_kernel,
        grid=(p, A_HEADS),
        in_specs=[
            pl.BlockSpec(memory_space=pltpu.SMEM),
            pl.BlockSpec((1, BAND, 2 * BAND), lambda i, h: (i, 0, 0)),
        ],
        out_specs=pl.BlockSpec((1, 1, BAND, 2 * BAND), lambda i, h: (i, h, 0, 0)),
        out_shape=jax.ShapeDtypeStruct((p, A_HEADS, BAND, 2 * BAND), F32),
        compiler_params=_params(("arbitrary", "arbitrary")),
        name="attn_bias",
    )(rel_bias.astype(F32), buckets)


def _attn_kernel(q_ref, k_ref, v_ref, qg_ref, kg_ref, tab_ref, o_ref,
                 qn, kn, vf, qp, kp0, kp1, vp, mrun, lrun, arun):
    s = q_ref.shape[0]
    w = BAND
    lane = lax.broadcasted_iota(I32, (1, LANES), 1)
    h0 = lane < HEAD_DIM
    rows = 256

    def norm_body(c, carry):
        r0 = pl.multiple_of(c * rows, rows)
        for src, dst, g in ((q_ref, qn, qg_ref), (k_ref, kn, kg_ref)):
            t = src[pl.ds(r0, rows), :].astype(F32)
            sq = t * t
            s0 = jnp.sum(jnp.where(h0, sq, 0.0), axis=-1, keepdims=True)
            s1 = jnp.sum(jnp.where(h0, 0.0, sq), axis=-1, keepdims=True)
            ms = jnp.where(h0, s0, s1) * (1.0 / HEAD_DIM)
            dst[pl.ds(r0, rows), :] = t * lax.rsqrt(ms + EPS) * g[...]
        vf[pl.ds(r0, rows), :] = v_ref[pl.ds(r0, rows), :].astype(F32)
        return carry

    lax.fori_loop(0, s // rows, norm_body, 0)

    mrun[...] = jnp.full(mrun.shape, NEG, F32)
    lrun[...] = jnp.zeros(lrun.shape, F32)
    arun[...] = jnp.zeros(arun.shape, F32)
    zpad = jnp.zeros((w, LANES), BF16)
    kp0[0:w, :] = zpad
    kp1[0:w, :] = zpad
    vp[0:w, :] = zpad

    col = lax.broadcasted_iota(I32, (1, 2 * w), 1)

    for p, (window, dil) in enumerate(A_PATTERNS):
        seg = s // dil
        nb = seg // w
        nblk = s // w

        def src_index(start):
            if dil == 1:
                return pl.ds(start, w)
            return pl.ds(start, w, stride=dil)

        for r in range(dil):
            for c in range(nb):
                idx = src_index(r + dil * w * c)
                dst = w + r * seg + c * w
                kv = kn[idx, :]
                qp[dst:dst + w, :] = qn[idx, :].astype(BF16)
                kp0[dst:dst + w, :] = jnp.where(h0, kv, 0.0).astype(BF16)
                kp1[dst:dst + w, :] = jnp.where(h0, 0.0, kv).astype(BF16)
                vp[dst:dst + w, :] = vf[idx, :].astype(BF16)

        for i in range(nblk):
            row0 = i * w
            qb = qp[row0 + w:row0 + 2 * w, :]
            vw = vp[row0:row0 + 2 * w, :]
            n = i % nb
            r = i // nb
            res = []
            for h, kp in ((0, kp0), (1, kp1)):
                kw = kp[row0:row0 + 2 * w, :]
                lg = _nt(qb, kw) + tab_ref[p, h]
                if n == 0:
                    lg = jnp.where(col >= w, lg, NEG)
                m = jnp.max(lg, axis=-1, keepdims=True)
                pe = jnp.exp(lg - m)
                l = jnp.sum(pe, axis=-1, keepdims=True)
                pv = _mm(pe.astype(BF16), vw)
                res.append((m, l, pv))
            mc = jnp.where(h0, res[0][0], res[1][0])
            lc = jnp.where(h0, res[0][1], res[1][1])
            ac = jnp.where(h0, res[0][2], res[1][2])
            tok = src_index(r + dil * w * n)
            mo = mrun[tok, :]
            mn = jnp.maximum(mo, mc)
            ea = jnp.exp(mo - mn)
            eb = jnp.exp(mc - mn)
            lrun[tok, :] = ea * lrun[tok, :] + eb * lc
            arun[tok, :] = ea * arun[tok, :] + eb * ac
            mrun[tok, :] = mn

    def out_body(c, carry):
        r0 = pl.multiple_of(c * rows, rows)
        o_ref[pl.ds(r0, rows), :] = (arun[pl.ds(r0, rows), :] / lrun[pl.ds(r0, rows), :]).astype(o_ref.dtype)
        return carry

    lax.fori_loop(0, s // rows, out_body, 0)


def _attention(qkv, q_gain, k_gain, tab, batch, seq):
    t = qkv.shape[0]
    npair = A_HEADS // 2
    qg = (jnp.tile(q_gain.astype(F32), 2) * HEAD_DIM ** -0.5).reshape(1, LANES)
    kg = jnp.tile(k_gain.astype(F32), 2).reshape(1, LANES)
    spad = seq + BAND
    return pl.pallas_call(
        _attn_kernel,
        grid=(batch, npair),
        in_specs=[
            pl.BlockSpec((seq, LANES), lambda b, h: (b, h)),
            pl.BlockSpec((seq, LANES), lambda b, h: (b, npair + h)),
            pl.BlockSpec((seq, LANES), lambda b, h: (b, 2 * npair + h)),
            pl.BlockSpec((1, LANES), lambda b, h: (0, 0)),
            pl.BlockSpec((1, LANES), lambda b, h: (0, 0)),
            pl.BlockSpec((len(A_PATTERNS), 2, BAND, 2 * BAND), lambda b, h: (0, h, 0, 0)),
        ],
        out_specs=pl.BlockSpec((seq, LANES), lambda b, h: (b, h)),
        out_shape=jax.ShapeDtypeStruct((t, A_WIDTH), BF16),
        scratch_shapes=[
            pltpu.VMEM((seq, LANES), F32),
            pltpu.VMEM((seq, LANES), F32),
            pltpu.VMEM((seq, LANES), F32),
            pltpu.VMEM((spad, LANES), BF16),
            pltpu.VMEM((spad, LANES), BF16),
            pltpu.VMEM((spad, LANES), BF16),
            pltpu.VMEM((spad, LANES), BF16),
            pltpu.VMEM((seq, LANES), F32),
            pltpu.VMEM((seq, LANES), F32),
            pltpu.VMEM((seq, LANES), F32),
        ],
        compiler_params=_params(("parallel", "parallel")),
        name="dilated_attn",
    )(qkv, qkv, qkv, qg, kg, tab)


def _hgrn_consts():
    c = B_CHUNK
    hk = B_QK_WIDTH
    hv = B_WIDTH
    grp = 16
    tri = np.tril(np.ones((c, c), np.float32))
    gt = np.zeros((B_HEADS * grp, grp * hk), np.float32)
    for h in range(B_HEADS):
        for j in range(grp):
            gt[h * grp + j, j * hk + h * B_KEY_DIM:j * hk + (h + 1) * B_KEY_DIM] = 1.0
    bd = np.zeros((hv, hk), np.float32)
    vm = np.zeros((B_HEADS * grp, hv), np.float32)
    hm = np.zeros((hv, hv), np.float32)
    for h in range(B_HEADS):
        bd[h * B_VAL_DIM:(h + 1) * B_VAL_DIM, h * B_KEY_DIM:(h + 1) * B_KEY_DIM] = 1.0
        vm[h * grp:(h + 1) * grp, h * B_VAL_DIM:(h + 1) * B_VAL_DIM] = 1.0
        hm[h * B_VAL_DIM:(h + 1) * B_VAL_DIM, h * B_VAL_DIM:(h + 1) * B_VAL_DIM] = 1.0
    return tri, gt, bd, vm, hm, grp


def _hgrn_kernel(qb_ref, fb_ref, ib_ref, og_ref, lbl_ref, onorm_ref, tri_ref, gt_ref, bd_ref, vm_ref, hm_ref,
                 o_ref, st, dstack, bsc, qsc, ksc, *, grp, layer):
    s = qb_ref.shape[0]
    c = B_CHUNK
    hk = B_QK_WIDTH
    ngrp = c // grp

    st[...] = jnp.zeros(st.shape, F32)
    dstack[...] = jnp.zeros(dstack.shape, BF16)
    lbl = lbl_ref[...]
    e = jnp.exp(lbl - jnp.max(lbl, axis=0, keepdims=True))
    lb = jnp.sum(e[0:layer + 1, :], axis=0, keepdims=True) / jnp.sum(e, axis=0, keepdims=True)
    tri = tri_ref[...]

    def chunk(ci, carry):
        c0 = pl.multiple_of(ci * c, c)
        qr = qb_ref[pl.ds(c0, c), :]
        q = qr * jax.nn.sigmoid(qr) * (B_KEY_DIM ** -0.5)
        f = lb + (1.0 - lb) * jax.nn.sigmoid(fb_ref[pl.ds(c0, c), :])
        g1, g2, g3 = _split3(jnp.log(f))
        bcum = _mm(tri, g1) + _mm(tri, g2) + _mm(tri, g3)
        kk = 1.0 - f
        v = ib_ref[pl.ds(c0, c), :]
        vb = v.astype(BF16)

        stv = st[...]
        o = _nt((q * jnp.exp(bcum)).astype(BF16), stv.astype(BF16))
        blast = bcum[c - 1:c, :]
        kd = kk * jnp.exp(blast - bcum)
        st[...] = stv * jnp.exp(blast) + _tn(vb, kd.astype(BF16)) * bd_ref[...]

        bsc[...] = bcum
        qsc[...] = q
        ksc[...] = kk
        for sp in range(c):
            t0 = (sp // 16) * 16
            n = c - t0
            dif = bsc[t0:c, :] - bsc[sp:sp + 1, :]
            trow = lax.broadcasted_iota(I32, (n, 1), 0) + t0
            ex = jnp.exp(jnp.where(trow >= sp, dif, NEG))
            dv = (qsc[t0:c, :] * ksc[sp:sp + 1, :]) * ex
            dstack[t0:c, sp * hk:(sp + 1) * hk] = dv.astype(BF16)

        for gi in range(ngrp):
            sc = _nt(dstack[:, gi * grp * hk:(gi + 1) * grp * hk], gt_ref[...])
            vg = v[gi * grp:(gi + 1) * grp, :]
            vg = (jnp.concatenate([vg] * B_HEADS, axis=0) * vm_ref[...]).astype(BF16)
            o = o + _mm(sc.astype(BF16), vg)

        o2 = o * o
        o2h = o2.astype(BF16)
        o2l = (o2 - o2h.astype(F32)).astype(BF16)
        ms = (_mm(o2h, hm_ref[...]) + _mm(o2l, hm_ref[...])) * (1.0 / B_VAL_DIM)
        ogr = og_ref[pl.ds(c0, c), :]
        y = o * lax.rsqrt(ms + EPS) * onorm_ref[...] * (ogr * jax.nn.sigmoid(ogr))
        o_ref[pl.ds(c0, c), :] = y.astype(o_ref.dtype)
        return carry

    lax.fori_loop(0, s // c, chunk, 0)


def _hgrn(qf, io, lb_logits, layer, out_norm, batch, seq):
    t = qf.shape[0]
    tri, gt, bd, vm, hm, grp = _hgrn_consts()
    nslot = lb_logits.shape[0]
    const = lambda a, dt: jnp.asarray(a, dt)
    full = lambda shp: pl.BlockSpec(shp, lambda b: (0,) * len(shp))
    return pl.pallas_call(
        functools.partial(_hgrn_kernel, grp=grp, layer=layer),
        grid=(batch,),
        in_specs=[
            pl.BlockSpec((seq, B_QK_WIDTH), lambda b: (b, 0)),
            pl.BlockSpec((seq, B_QK_WIDTH), lambda b: (b, 1)),
            pl.BlockSpec((seq, B_WIDTH), lambda b: (b, 0)),
            pl.BlockSpec((seq, B_WIDTH), lambda b: (b, 1)),
            full((nslot, B_QK_WIDTH)),
            full((1, B_WIDTH)),
            full(tri.shape), full(gt.shape), full(bd.shape), full(vm.shape), full(hm.shape),
        ],
        out_specs=pl.BlockSpec((seq, B_WIDTH), lambda b: (b, 0)),
        out_shape=jax.ShapeDtypeStruct((t, B_WIDTH), BF16),
        scratch_shapes=[
            pltpu.VMEM((B_WIDTH, B_QK_WIDTH), F32),
            pltpu.VMEM((B_CHUNK, B_CHUNK * B_QK_WIDTH), BF16),
            pltpu.VMEM((B_CHUNK, B_QK_WIDTH), F32),
            pltpu.VMEM((B_CHUNK, B_QK_WIDTH), F32),
            pltpu.VMEM((B_CHUNK, B_QK_WIDTH), F32),
        ],
        compiler_params=_params(("parallel",)),
        name="hgrn2",
    )(qf, qf, io, io, lb_logits.astype(F32), out_norm.astype(F32).reshape(1, B_WIDTH),
      const(tri, BF16), const(gt, BF16), const(bd, F32), const(vm, F32), const(hm, BF16))


def _head_ms(t, masks):
    sq = t * t
    ms = jnp.zeros_like(t)
    for mk in masks:
        ms = jnp.where(mk, jnp.sum(jnp.where(mk, sq, 0.0), axis=-1, keepdims=True), ms)
    return ms * (1.0 / HEAD_DIM)


def _mem_kernel(mem_ref, mg_ref, wkv_ref, q_ref, qg_ref, kg_ref, o_ref, ksc, vsc):
    s = q_ref.shape[0]
    lane = lax.broadcasted_iota(I32, (1, C_WIDTH), 1)
    masks = [(lane >= h * HEAD_DIM) & (lane < (h + 1) * HEAD_DIM) for h in range(C_HEADS)]

    m = mem_ref[0]
    ms = jnp.mean(m * m, axis=-1, keepdims=True)
    mn = (m * lax.rsqrt(ms + EPS) * mg_ref[...]).astype(BF16)
    kv = _mm(mn, wkv_ref[...])
    km = kv[:, :C_WIDTH]
    k = km * lax.rsqrt(_head_ms(km, masks) + EPS) * kg_ref[...]
    for h in range(C_HEADS):
        ksc[h] = jnp.where(masks[h], k, 0.0).astype(BF16)
    vsc[...] = kv[:, C_WIDTH:].astype(BF16)

    rows = 256

    def body(c, carry):
        r0 = pl.multiple_of(c * rows, rows)
        q = q_ref[pl.ds(r0, rows), :].astype(F32)
        qn = (q * lax.rsqrt(_head_ms(q, masks) + EPS) * qg_ref[...]).astype(BF16)
        out = jnp.zeros((rows, C_WIDTH), F32)
        for h in range(C_HEADS):
            lg = _nt(qn, ksc[h])
            mx = jnp.max(lg, axis=-1, keepdims=True)
            pe = jnp.exp(lg - mx)
            l = jnp.sum(pe, axis=-1, keepdims=True)
            out = jnp.where(masks[h], _mm(pe.astype(BF16), vsc[...]) / l, out)
        o_ref[pl.ds(r0, rows), :] = out.astype(o_ref.dtype)
        return carry

    lax.fori_loop(0, s // rows, body, 0)


def _memory(mem, mem_g, wkv_bf16, qc, q_gain, k_gain, seq):
    batch, mlen, d = mem.shape
    t = qc.shape[0]
    qg = (jnp.tile(q_gain.astype(F32), C_HEADS) * HEAD_DIM ** -0.5).reshape(1, C_WIDTH)
    kg = jnp.tile(k_gain.astype(F32), C_HEADS).reshape(1, C_WIDTH)
    return pl.pallas_call(
        _mem_kernel,
        grid=(batch,),
        in_specs=[
            pl.BlockSpec((1, mlen, d), lambda b: (b, 0, 0)),
            pl.BlockSpec((1, d), lambda b: (0, 0)),
            pl.BlockSpec(wkv_bf16.shape, lambda b: (0, 0)),
            pl.BlockSpec((seq, C_WIDTH), lambda b: (b, 0)),
            pl.BlockSpec((1, C_WIDTH), lambda b: (0, 0)),
            pl.BlockSpec((1, C_WIDTH), lambda b: (0, 0)),
        ],
        out_specs=pl.BlockSpec((seq, C_WIDTH), lambda b: (b, 0)),
        out_shape=jax.ShapeDtypeStruct((t, C_WIDTH), BF16),
        scratch_shapes=[
            pltpu.VMEM((C_HEADS, mlen, C_WIDTH), BF16),
            pltpu.VMEM((mlen, C_WIDTH), BF16),
        ],
        compiler_params=_params(("parallel",)),
        name="mem_xattn",
    )(mem, mem_g.astype(F32).reshape(1, d), wkv_bf16, qc, qg, kg)


def _merge_kernel(x_ref, ya_ref, yb_ref, yc_ref, gate_ref, wa_ref, wb_ref, wc_ref, wo_ref, fg_ref,
                  wrh_ref, wrl_ref, wsu_ref, wsd_ref, x2_ref, up_ref, st_ref):
    d = x_ref.shape[1]
    mixed = jnp.zeros(x_ref.shape, F32)
    for i, (y_ref, w_ref) in enumerate(((ya_ref, wa_ref), (yb_ref, wb_ref), (yc_ref, wc_ref))):
        g = jax.nn.sigmoid(gate_ref[:, i * d:(i + 1) * d].astype(F32))
        mixed = mixed + g * _mm(y_ref[...], w_ref[...])
    x1 = x_ref[...] + _mm(mixed.astype(BF16), wo_ref[...])
    ms = jnp.mean(x1 * x1, axis=-1, keepdims=True)
    u = x1 * lax.rsqrt(ms + EPS) * fg_ref[...]
    ub = u.astype(BF16)
    ul = (u - ub.astype(F32)).astype(BF16)
    logits = _nt(wrh_ref[...], ub) + _nt(wrl_ref[...], ub) + _nt(wrh_ref[...], ul)
    st_ref[...] = jax.nn.sigmoid(logits)
    hid = _mm(ub, wsu_ref[...])
    nh = hid.shape[1] // 2
    act = (jax.nn.silu(hid[:, :nh]) * hid[:, nh:]).astype(BF16)
    x2_ref[...] = x1 + _mm(act, wsd_ref[...])
    up_ref[...] = ub


def _merge(x2d, ya, yb, yc, gates, wa, wb, wc, wo, fg, wr_hi, wr_lo, wsu, wsd):
    t, d = x2d.shape
    tm = TM_PROJ
    row = lambda w: pl.BlockSpec((tm, w), lambda i: (i, 0))
    full = lambda a: pl.BlockSpec(a.shape, lambda i: (0,) * a.ndim)
    fg2 = fg.astype(F32).reshape(1, d)
    return pl.pallas_call(
        _merge_kernel,
        grid=(t // tm,),
        in_specs=[row(d), row(A_WIDTH), row(B_WIDTH), row(C_WIDTH), row(3 * d),
                  full(wa), full(wb), full(wc), full(wo), full(fg2), full(wr_hi), full(wr_lo),
                  full(wsu), full(wsd)],
        out_specs=[row(d), row(d), pl.BlockSpec((N_EXPERTS, tm), lambda i: (0, i))],
        out_shape=[jax.ShapeDtypeStruct((t, d), F32),
                   jax.ShapeDtypeStruct((t, d), BF16),
                   jax.ShapeDtypeStruct((N_EXPERTS, t), F32)],
        compiler_params=_params(("parallel",)),
        name="merge_router_shared",
    )(x2d, ya, yb, yc, gates, wa, wb, wc, wo, fg2, wr_hi, wr_lo, wsu, wsd)


def _row_to_expert():
    per_group = N_EXPERTS // N_GROUPS
    rows = np.arange(N_EXPERTS)
    return (rows % N_GROUPS) * per_group + rows // N_GROUPS


def _route_kernel(s_ref, bias_ref, eidx_ref, ut_ref, ones_ref, ls_ref,
                  idx_ref, w_ref, lpos_ref, seg_ref, loff_ref, car_ref, carry):
    tr = s_ref.shape[1]
    per_group = N_EXPERTS // N_GROUPS
    ng = N_GROUPS
    nrep = tr // LANES
    ninf = -jnp.inf

    @pl.when(pl.program_id(0) == 0)
    def _():
        carry[...] = jnp.zeros(carry.shape, F32)

    s_all = s_ref[...]
    bias = jnp.concatenate([bias_ref[...]] * nrep, axis=1)
    eidx = jnp.concatenate([eidx_ref[...]] * nrep, axis=1)
    sv = [s_all[i * ng:(i + 1) * ng, :] for i in range(per_group)]
    ev = [eidx[i * ng:(i + 1) * ng, :] for i in range(per_group)]
    cv = [sv[i] + bias[i * ng:(i + 1) * ng, :] for i in range(per_group)]

    def vmax(xs):
        out = xs[0]
        for x in xs[1:]:
            out = jnp.maximum(out, x)
        return out

    def vmin(xs):
        out = xs[0]
        for x in xs[1:]:
            out = jnp.minimum(out, x)
        return out

    m1 = vmax(cv)
    i1 = vmin([jnp.where(cv[i] == m1, i, per_group) for i in range(per_group)])
    m2 = vmax([jnp.where(i1 == i, ninf, cv[i]) for i in range(per_group)])
    gs = m1 + m2
    giota = lax.broadcasted_iota(I32, (ng, tr), 0)
    gmask = jnp.zeros((ng, tr), jnp.bool_)
    for _ in range(TOPK_GROUPS):
        mx = jnp.max(gs, axis=0, keepdims=True)
        pick = jnp.min(jnp.where(gs == mx, giota, ng), axis=0, keepdims=True)
        sel = giota == pick
        gmask = jnp.logical_or(gmask, sel)
        gs = jnp.where(sel, ninf, gs)

    cm = [jnp.where(gmask, cv[i], ninf) for i in range(per_group)]
    idx_out = jnp.zeros((TOP_K, tr), I32)
    w_out = jnp.zeros((TOP_K, tr), F32)
    jiota = lax.broadcasted_iota(I32, (TOP_K, tr), 0)
    for j in range(TOP_K):
        mx = jnp.max(vmax(cm), axis=0, keepdims=True)
        emin = jnp.min(vmin([jnp.where(cm[i] == mx, ev[i], N_EXPERTS) for i in range(per_group)]),
                       axis=0, keepdims=True)
        hit = [ev[i] == emin for i in range(per_group)]
        ssel = hit_sum = None
        for i in range(per_group):
            term = jnp.where(hit[i], sv[i], 0.0)
            hit_sum = term if hit_sum is None else hit_sum + term
        ssel = jnp.sum(hit_sum, axis=0, keepdims=True)
        cm = [jnp.where(hit[i], ninf, cm[i]) for i in range(per_group)]
        idx_out = jnp.where(jiota == j, emin, idx_out)
        w_out = jnp.where(jiota == j, ssel, w_out)
    idx_ref[...] = idx_out
    w_ref[...] = ROUTED_SCALE * w_out / jnp.sum(w_out, axis=0, keepdims=True)

    chosen = [jnp.where(jnp.logical_and(gmask, cm[i] == ninf), 1.0, 0.0) for i in range(per_group)]
    mh = jnp.concatenate(chosen, axis=0).astype(BF16)
    cnt = _mm(mh, ones_ref[...])
    seg8 = ((cnt.astype(I32) + (SUBLANES - 1)) // SUBLANES) * SUBLANES
    seg8f = seg8.astype(F32)
    loff = _mm(ls_ref[...], seg8f.astype(BF16))
    lfull = _mm(mh, ut_ref[...]) + jnp.concatenate([loff] * nrep, axis=1)
    lv = [lfull[i * ng:(i + 1) * ng, :] for i in range(per_group)]
    lpos = jnp.zeros((TOP_K, tr), F32)
    for j in range(TOP_K):
        ej = idx_out[j:j + 1, :]
        acc = None
        for i in range(per_group):
            term = jnp.where(ev[i] == ej, lv[i], 0.0)
            acc = term if acc is None else acc + term
        lpos = jnp.where(jiota == j, jnp.sum(acc, axis=0, keepdims=True), lpos)
    lpos_ref[...] = lpos.astype(I32)
    cr = carry[...]
    seg_ref[0] = seg8
    loff_ref[0] = loff.astype(I32)
    car_ref[0] = cr.astype(I32)
    carry[...] = cr + seg8f


def _route(s_t, bias_rows, t):
    tr = TB_MOE
    nt = t // tr
    r2e = _row_to_expert()
    eidx = jnp.asarray(np.broadcast_to(r2e[:, None], (N_EXPERTS, LANES)).astype(np.int32))
    bias = jnp.broadcast_to(bias_rows.astype(F32)[:, None], (N_EXPERTS, LANES))
    ut = jnp.asarray(np.triu(np.ones((tr, tr), np.float32), k=1), BF16)
    ls = jnp.asarray(np.tril(np.ones((N_EXPERTS, N_EXPERTS), np.float32), k=-1), BF16)
    ones = jnp.ones((tr, LANES), BF16)
    full = lambda a: pl.BlockSpec(a.shape, lambda i: (0,) * a.ndim)
    tile = pl.BlockSpec((TOP_K, tr), lambda i: (0, i))
    tab = pl.BlockSpec((1, N_EXPERTS, LANES), lambda i: (i, 0, 0))
    tab_shape = jax.ShapeDtypeStruct((nt, N_EXPERTS, LANES), I32)
    return pl.pallas_call(
        _route_kernel,
        grid=(nt,),
        in_specs=[pl.BlockSpec((N_EXPERTS, tr), lambda i: (0, i)), full(bias), full(eidx), full(ut), full(ones),
                  full(ls)],
        out_specs=[tile, tile, tile, tab, tab, tab],
        out_shape=[jax.ShapeDtypeStruct((TOP_K, t), I32),
                   jax.ShapeDtypeStruct((TOP_K, t), F32),
                   jax.ShapeDtypeStruct((TOP_K, t), I32),
                   tab_shape, tab_shape, tab_shape],
        scratch_shapes=[pltpu.VMEM((N_EXPERTS, LANES), F32)],
        compiler_params=_params(("arbitrary",)),
        name="route_topk_layout",
    )(s_t, bias, eidx, ut, ones, ls)


def _piece_copies(grow_ref, used_ref, tile, local_ref, hbm_ref, sem, to_hbm):
    base = tile * PIECES_MOE

    def piece(p, carry):
        loc = local_ref.at[pl.ds(pl.multiple_of(p * SUBLANES, SUBLANES), SUBLANES)]
        glb = hbm_ref.at[pl.ds(pl.multiple_of(grow_ref[base + p], SUBLANES), SUBLANES)]
        if to_hbm:
            pltpu.make_async_copy(loc, glb, sem).start()
        else:
            pltpu.make_async_copy(glb, loc, sem).start()
        return carry

    lax.fori_loop(0, used_ref[tile], piece, 0)


def _wait_rows(n8, local_ref, hbm_ref, sem):
    for bit in USED_BITS:
        @pl.when((n8 & bit) != 0)
        def _():
            size = SUBLANES * bit
            pltpu.make_async_copy(local_ref.at[pl.ds(0, size)], hbm_ref.at[pl.ds(0, size)], sem).wait()


def _dispatch_kernel(grow_ref, used_ref, zrow_ref, nused_ref, lpos_ref, u_ref, xs_hbm,
                     xloc, zbuf, sems, zsem):
    i = pl.program_id(0)
    nt = pl.num_programs(0)
    tb = u_ref.shape[0]
    bm = zbuf.shape[0]
    slot = lax.rem(i, 2)

    def zero_copy(e):
        return pltpu.make_async_copy(zbuf, xs_hbm.at[pl.ds(pl.multiple_of(zrow_ref[e], bm), bm)], zsem)

    @pl.when(i == 0)
    def _():
        zbuf[...] = jnp.zeros(zbuf.shape, zbuf.dtype)

        def zstart(e, carry):
            @pl.when(zrow_ref[e] >= 0)
            def _():
                zero_copy(e).start()
            return carry

        def zwait(e, carry):
            @pl.when(zrow_ref[e] >= 0)
            def _():
                zero_copy(e).wait()
            return carry

        lax.fori_loop(0, N_EXPERTS, zstart, 0)
        lax.fori_loop(0, N_EXPERTS, zwait, 0)

        def tail_copy(b):
            return pltpu.make_async_copy(zbuf, xs_hbm.at[pl.ds(pl.multiple_of(b * bm, bm), bm)], zsem)

        def tstart(b, carry):
            tail_copy(b).start()
            return carry

        def twait(b, carry):
            tail_copy(b).wait()
            return carry

        nblk = xs_hbm.shape[0] // bm
        lax.fori_loop(nused_ref[0], nblk, tstart, 0)
        lax.fori_loop(nused_ref[0], nblk, twait, 0)

    @pl.when(i >= 2)
    def _():
        _wait_rows(used_ref[i - 2], xloc.at[slot], xs_hbm, sems.at[slot])

    used = used_ref[i]
    ub = u_ref[...]
    for rc in range(RL_MOE // RC_MOE):
        @pl.when(rc * (RC_MOE // SUBLANES) < used)
        def _():
            rows = lax.broadcasted_iota(I32, (RC_MOE, tb), 0) + rc * RC_MOE
            onehot = jnp.zeros((RC_MOE, tb), F32)
            for j in range(TOP_K):
                onehot = jnp.where(rows == lpos_ref[j:j + 1, :], 1.0, onehot)
            xloc[slot, rc * RC_MOE:(rc + 1) * RC_MOE, :] = _pack_pairs(_mm(onehot.astype(BF16), ub))

    _piece_copies(grow_ref, used_ref, i, xloc.at[slot], xs_hbm, sems.at[slot], True)

    @pl.when(i == nt - 1)
    def _():
        _wait_rows(used, xloc.at[slot], xs_hbm, sems.at[slot])

        @pl.when(i >= 1)
        def _():
            _wait_rows(used_ref[i - 1], xloc.at[1 - slot], xs_hbm, sems.at[1 - slot])


def _dispatch(tables, zrow, n_used, lpos_t, u_bf16, n_rows):
    t, d = u_bf16.shape
    tb = TB_MOE
    half = d // 2
    nsp = len(tables) + 2
    return pl.pallas_call(
        _dispatch_kernel,
        grid_spec=pltpu.PrefetchScalarGridSpec(
            num_scalar_prefetch=nsp,
            grid=(t // tb,),
            in_specs=[pl.BlockSpec((TOP_K, tb), lambda i, *_: (0, i)),
                      pl.BlockSpec((tb, d), lambda i, *_: (i, 0))],
            out_specs=pl.BlockSpec(memory_space=pl.ANY),
            scratch_shapes=[
                pltpu.VMEM((2, RL_MOE, half), U32),
                pltpu.VMEM((BM_EXPERT, half), U32),
                pltpu.SemaphoreType.DMA((2,)),
                pltpu.SemaphoreType.DMA,
            ],
        ),
        out_shape=jax.ShapeDtypeStruct((n_rows, half), U32),
        compiler_params=_params(("arbitrary",)),
        name="dispatch_rows",
    )(*tables, zrow, n_used, lpos_t, u_bf16)


def _expert_kernel(be_ref, nu_ref, xs_ref, wu_ref, wd_ref, ys_ref):
    @pl.when(pl.program_id(0) < nu_ref[0])
    def _():
        xa, xb = _unpack_pairs(xs_ref[...])
        half = xa.shape[1]
        hid = _mm(xa.astype(BF16), wu_ref[0, :half, :]) + _mm(xb.astype(BF16), wu_ref[0, half:, :])
        nh = hid.shape[1] // 2
        act = (jax.nn.silu(hid[:, :nh]) * hid[:, nh:]).astype(BF16)
        ys_ref[...] = _pack_pairs(_mm(act, wd_ref[0]))

    @pl.when(pl.program_id(0) >= nu_ref[0])
    def _():
        ys_ref[...] = jnp.zeros(ys_ref.shape, ys_ref.dtype)


def _experts(block_expert, n_used, xs, wu, wd):
    n_rows, half = xs.shape
    bm = BM_EXPERT
    nblk = n_rows // bm
    d = 2 * half
    last = lambda b, nu: jnp.minimum(b, nu[0] - 1)
    return pl.pallas_call(
        _expert_kernel,
        grid_spec=pltpu.PrefetchScalarGridSpec(
            num_scalar_prefetch=2,
            grid=(nblk,),
            in_specs=[
                pl.BlockSpec((bm, half), lambda b, be, nu: (last(b, nu), 0)),
                pl.BlockSpec((1, d, wu.shape[2]), lambda b, be, nu: (be[last(b, nu)], 0, 0)),
                pl.BlockSpec((1, wd.shape[1], d), lambda b, be, nu: (be[last(b, nu)], 0, 0)),
            ],
            out_specs=pl.BlockSpec((bm, half), lambda b, be, nu: (b, 0)),
        ),
        out_shape=jax.ShapeDtypeStruct((n_rows, half), U32),
        compiler_params=_params(("arbitrary",)),
        name="expert_swiglu",
    )(block_expert, n_used, xs, wu, wd)


def _combine_kernel(grow_ref, used_ref, ys_hbm, lpos_ref, w_ref, x2_ref, o_ref,
                    yloc, acc, sems):
    i = pl.program_id(0)
    nt = pl.num_programs(0)
    tb = x2_ref.shape[0]
    half = yloc.shape[2]
    slot = lax.rem(i, 2)

    def fetch(tile, sl):
        _piece_copies(grow_ref, used_ref, tile, yloc.at[sl], ys_hbm, sems.at[sl], False)

    @pl.when(i == 0)
    def _():
        yloc[...] = jnp.zeros(yloc.shape, yloc.dtype)
        fetch(0, 0)

    @pl.when(i + 1 < nt)
    def _():
        fetch(i + 1, 1 - slot)

    used = used_ref[i]
    _wait_rows(used, yloc.at[slot], ys_hbm, sems.at[slot])

    acc[...] = x2_ref[...]
    for kc in range(RL_MOE // RC_MOE):
        @pl.when(kc * (RC_MOE // SUBLANES) < used)
        def _():
            rows = lax.broadcasted_iota(I32, (RC_MOE, tb), 0) + kc * RC_MOE
            pw = jnp.zeros((RC_MOE, tb), F32)
            for j in range(TOP_K):
                pw = jnp.where(rows == lpos_ref[j:j + 1, :], w_ref[j:j + 1, :], pw)
            pw = pw.astype(BF16)
            ya, yb = _unpack_pairs(yloc[slot, kc * RC_MOE:(kc + 1) * RC_MOE, :])
            acc[:, :half] += _tn(pw, ya.astype(BF16))
            acc[:, half:] += _tn(pw, yb.astype(BF16))
    o_ref[...] = acc[...]


def _combine(tables, ys, lpos_t, w_t, x2):
    t, d = x2.shape
    tb = TB_MOE
    half = d // 2
    nsp = len(tables)
    tok = lambda w: pl.BlockSpec((tb, w), lambda i, *_: (i, 0))
    slots = pl.BlockSpec((TOP_K, tb), lambda i, *_: (0, i))
    return pl.pallas_call(
        _combine_kernel,
        grid_spec=pltpu.PrefetchScalarGridSpec(
            num_scalar_prefetch=nsp,
            grid=(t // tb,),
            in_specs=[pl.BlockSpec(memory_space=pl.ANY), slots, slots, tok(d)],
            out_specs=tok(d),
            scratch_shapes=[
                pltpu.VMEM((2, RL_MOE, half), U32),
                pltpu.VMEM((tb, d), F32),
                pltpu.SemaphoreType.DMA((2,)),
            ],
        ),
        out_shape=jax.ShapeDtypeStruct((t, d), F32),
        compiler_params=_params(("arbitrary",)),
        name="combine_rows",
    )(*tables, ys, lpos_t, w_t, x2)


def kernel(x, mem, attn_norm_g, w_in, q_norm_a, k_norm_a, rel_bias, lb_logits, out_norm_b, mem_norm_g, w_mem_kv,
           q_norm_c, k_norm_c, w_branch_a, w_branch_b, w_branch_c, w_out, ffn_norm_g, w_router, router_bias,
           w_exp_up, w_exp_down, w_shared_up, w_shared_down):
    batch, seq, d = x.shape
    t = batch * seq
    depth = w_in.shape[0]
    tab = _bias_tables(rel_bias)
    r2e = _row_to_expert()
    xf = x.reshape(t, d)
    for layer in range(depth):
        qkv, qf, io, qc, gates = _inproj(xf, attn_norm_g[layer], w_in[layer].astype(BF16))
        ya = _attention(qkv, q_norm_a[layer], k_norm_a[layer], tab, batch, seq)
        yb = _hgrn(qf, io, lb_logits, layer, out_norm_b[layer], batch, seq)
        yc = _memory(mem, mem_norm_g[layer], w_mem_kv[layer].astype(BF16), qc, q_norm_c[layer], k_norm_c[layer], seq)

        wr_rows = w_router[layer].astype(F32).T[r2e]
        wr_hi = wr_rows.astype(BF16)
        wr_lo = (wr_rows - wr_hi.astype(F32)).astype(BF16)
        x2, u_bf16, s_t = _merge(
            xf, ya, yb, yc, gates,
            w_branch_a[layer].astype(BF16), w_branch_b[layer].astype(BF16), w_branch_c[layer].astype(BF16),
            w_out[layer].astype(BF16), ffn_norm_g[layer], wr_hi, wr_lo,
            w_shared_up[layer].astype(BF16), w_shared_down[layer].astype(BF16))

        _, w_t, lpos_t, seg_tab, loff_tab, car_tab = _route(s_t, router_bias[layer][r2e], t)

        bm = BM_EXPERT
        nt = t // TB_MOE
        seg = seg_tab[:, :, 0]
        loff = loff_tab[:, :, 0]
        car = car_tab[:, :, 0]
        total = car[-1] + seg[-1]
        padded = (total + bm - 1) // bm * bm
        pend = jnp.cumsum(padded)
        pstart = pend - padded
        n_rows = (t * TOP_K + N_EXPERTS * (SUBLANES - 1) * nt) // bm * bm + N_EXPERTS * bm
        nblk = n_rows // bm
        block_row = jnp.arange(nblk, dtype=I32) * bm
        block_slot = jnp.minimum(jnp.sum((pend[None, :] <= block_row[:, None]).astype(I32), axis=1), N_EXPERTS - 1)
        block_expert = jnp.asarray(r2e, I32)[block_slot]
        n_used = (pend[-1:] // bm).astype(I32)
        zrow = jnp.where(padded > total, pend - bm, -1).astype(I32)
        seg_end = (loff + seg) // SUBLANES
        piece = jnp.arange(PIECES_MOE, dtype=I32)
        owner = jnp.minimum(jnp.sum((seg_end[:, None, :] <= piece[None, :, None]).astype(I32), axis=2),
                            N_EXPERTS - 1)
        gdst = pstart[None, :] + car
        grow = (jnp.take_along_axis(gdst - loff, owner, axis=1) + piece[None, :] * SUBLANES)
        tables = (grow.reshape(-1).astype(I32), seg_end[:, -1].astype(I32))

        xs = _dispatch(tables, zrow, n_used, lpos_t, u_bf16, n_rows)
        ys = _experts(block_expert, n_used, xs, w_exp_up[layer].astype(BF16), w_exp_down[layer].astype(BF16))
        xf = _combine(tables, ys, lpos_t, w_t, x2)
    return xf.reshape(batch, seq, d)
```

```python
import functools
import math

import numpy as np
import jax
import jax.numpy as jnp
from jax import lax
from jax.experimental import pallas as pl
from jax.experimental.pallas import tpu as pltpu

F32 = jnp.float32
BF16 = jnp.bfloat16
I32 = jnp.int32
U32 = jnp.uint32

HEAD_DIM = 64
EPS = 1e-6
A_HEADS = 6
A_PATTERNS = ((128, 1), (512, 4), (2048, 16))
BAND = 128
N_BUCKETS = 32
REL_MAX_DIST = 2048
B_HEADS = 4
B_KEY_DIM = 64
B_VAL_DIM = 96
B_CHUNK = 64
C_HEADS = 4
N_EXPERTS = 64
TOP_K = 8
N_GROUPS = 8
TOPK_GROUPS = 4
EXPERT_HIDDEN = 256
ROUTED_SCALE = 2.5

A_WIDTH = A_HEADS * HEAD_DIM
B_QK_WIDTH = B_HEADS * B_KEY_DIM
B_WIDTH = B_HEADS * B_VAL_DIM
C_WIDTH = C_HEADS * HEAD_DIM

NEG = -1e30
HI_MASK = 0xFFFF0000

LANES = 128
SUBLANES = 8
VMEM_LIMIT = 56 * 1024 * 1024

TM_PROJ = 512
TB_MOE = 256
BM_EXPERT = 512
RC_MOE = 256
RL_MOE = -(-(TB_MOE * TOP_K + N_EXPERTS * (SUBLANES - 1)) // RC_MOE) * RC_MOE
PIECES_MOE = RL_MOE // SUBLANES
USED_BITS = (256, 128, 64, 32, 16, 8, 4, 2, 1)


def _params(sem, vmem=VMEM_LIMIT):
    return pltpu.CompilerParams(dimension_semantics=sem, vmem_limit_bytes=vmem)


def _nt(a, b):
    return lax.dot_general(a, b, (((1,), (1,)), ((), ())), preferred_element_type=F32)


def _tn(a, b):
    return lax.dot_general(a, b, (((0,), (0,)), ((), ())), preferred_element_type=F32)


def _mm(a, b):
    return jnp.dot(a, b, preferred_element_type=F32)


def _split3(t):
    t1 = t.astype(BF16)
    r1 = t - t1.astype(F32)
    t2 = r1.astype(BF16)
    t3 = (r1 - t2.astype(F32)).astype(BF16)
    return t1, t2, t3


def _pack_pairs(y):
    n = y.shape[1] // 2
    bits = pltpu.bitcast(y.astype(BF16).astype(F32), U32)
    return (bits[:, :n] & U32(HI_MASK)) | (bits[:, n:] >> U32(16))


def _unpack_pairs(p):
    hi = pltpu.bitcast(p & U32(HI_MASK), F32)
    lo = pltpu.bitcast(p << U32(16), F32)
    return hi, lo


def _inproj_kernel(x_ref, g_ref, w_ref, qkv_ref, qf_ref, io_ref, qc_ref, gate_ref):
    x = x_ref[...]
    ms = jnp.mean(x * x, axis=-1, keepdims=True)
    h = (x * lax.rsqrt(ms + EPS) * g_ref[...]).astype(BF16)
    col = 0
    for out in (qkv_ref, qf_ref, io_ref, qc_ref, gate_ref):
        width = out.shape[1]
        for a in range(0, width, 512):
            b = min(a + 512, width)
            out[:, a:b] = _mm(h, w_ref[:, col + a:col + b]).astype(out.dtype)
        col += width


def _inproj(x2d, g, w_bf16):
    t, d = x2d.shape
    widths = (3 * A_WIDTH, 2 * B_QK_WIDTH, 2 * B_WIDTH, C_WIDTH, 3 * d)
    dtypes = (BF16, F32, F32, BF16, BF16)
    tm = TM_PROJ
    return pl.pallas_call(
        _inproj_kernel,
        grid=(t // tm,),
        in_specs=[
            pl.BlockSpec((tm, d), lambda i: (i, 0)),
            pl.BlockSpec((1, d), lambda i: (0, 0)),
            pl.BlockSpec(w_bf16.shape, lambda i: (0, 0), pipeline_mode=pl.Buffered(1)),
        ],
        out_specs=[pl.BlockSpec((tm, w), lambda i: (i, 0)) for w in widths],
        out_shape=[jax.ShapeDtypeStruct((t, w), dt) for w, dt in zip(widths, dtypes)],
        compiler_params=_params(("parallel",)),
        name="inproj",
    )(x2d, g.reshape(1, d), w_bf16)


def _bucket_tables():
    w = BAND
    i = np.arange(w)[:, None]
    c = np.arange(2 * w)[None, :]
    steps = w + i - c
    in_band = (steps >= 0) & (steps <= w)
    max_exact = N_BUCKETS // 2
    tabs = []
    for _, dil in A_PATTERNS:
        d = (np.clip(steps, 0, w) * dil).astype(np.int32)
        ratio = np.maximum(d, 1).astype(np.float32) / np.float32(max_exact)
        large = max_exact + (np.log(ratio) / np.float32(math.log(REL_MAX_DIST / max_exact))
                             * np.float32(N_BUCKETS - max_exact)).astype(np.int32)
        bucket = np.where(d < max_exact, d, np.minimum(large, N_BUCKETS - 1))
        tabs.append(np.where(in_band, bucket, -1))
    return np.stack(tabs).astype(np.int32)


def _bias_kernel(rb_ref, bucket_ref, out_ref):
    h = pl.program_id(1)
    bk = bucket_ref[0]
    acc = jnp.full(bk.shape, NEG, F32)
    for b in range(N_BUCKETS):
        acc = jnp.where(bk == b, rb_ref[b, h], acc)
    out_ref[0, 0] = acc


def _bias_tables(rel_bias):
    buckets = jnp.asarray(_bucket_tables())
    p = len(A_PATTERNS)
    return pl.pallas_call(
        _bias_kernel,
        grid=(p, A_HEADS),
        in_specs=[
            pl.BlockSpec(memory_space=pltpu.SMEM),
            pl.BlockSpec((1, BAND, 2 * BAND), lambda i, h: (i, 0, 0)),
        ],
        out_specs=pl.BlockSpec((1, 1, BAND, 2 * BAND), lambda i, h: (i, h, 0, 0)),
        out_shape=jax.ShapeDtypeStruct((p, A_HEADS, BAND, 2 * BAND), F32),
        compiler_params=_params(("arbitrary", "arbitrary")),
        name="attn_bias",
    )(rel_bias.astype(F32), buckets)


def _attn_kernel(q_ref, k_ref, v_ref, qg_ref, kg_ref, tab_ref, o_ref,
                 qn, kn, vf, qp, kp0, kp1, vp, mrun, lrun, arun):
    s = q_ref.shape[0]
    w = BAND
    lane = lax.broadcasted_iota(I32, (1, LANES), 1)
    h0 = lane < HEAD_DIM
    rows = 256

    def norm_body(c, carry):
        r0 = pl.multiple_of(c * rows, rows)
        for src, dst, g in ((q_ref, qn, qg_ref), (k_ref, kn, kg_ref)):
            t = src[pl.ds(r0, rows), :].astype(F32)
            sq = t * t
            s0 = jnp.sum(jnp.where(h0, sq, 0.0), axis=-1, keepdims=True)
            s1 = jnp.sum(jnp.where(h0, 0.0, sq), axis=-1, keepdims=True)
            ms = jnp.where(h0, s0, s1) * (1.0 / HEAD_DIM)
            dst[pl.ds(r0, rows), :] = t * lax.rsqrt(ms + EPS) * g[...]
        vf[pl.ds(r0, rows), :] = v_ref[pl.ds(r0, rows), :].astype(F32)
        return carry

    lax.fori_loop(0, s // rows, norm_body, 0)

    mrun[...] = jnp.full(mrun.shape, NEG, F32)
    lrun[...] = jnp.zeros(lrun.shape, F32)
    arun[...] = jnp.zeros(arun.shape, F32)
    zpad = jnp.zeros((w, LANES), BF16)
    kp0[0:w, :] = zpad
    kp1[0:w, :] = zpad
    vp[0:w, :] = zpad

    col = lax.broadcasted_iota(I32, (1, 2 * w), 1)

    for p, (window, dil) in enumerate(A_PATTERNS):
        seg = s // dil
        nb = seg // w
        nblk = s // w

        def src_index(start):
            if dil == 1:
                return pl.ds(start, w)
            return pl.ds(start, w, stride=dil)

        for r in range(dil):
            for c in range(nb):
                idx = src_index(r + dil * w * c)
                dst = w + r * seg + c * w
                kv = kn[idx, :]
                qp[dst:dst + w, :] = qn[idx, :].astype(BF16)
                kp0[dst:dst + w, :] = jnp.where(h0, kv, 0.0).astype(BF16)
                kp1[dst:dst + w, :] = jnp.where(h0, 0.0, kv).astype(BF16)
                vp[dst:dst + w, :] = vf[idx, :].astype(BF16)

        for i in range(nblk):
            row0 = i * w
            qb = qp[row0 + w:row0 + 2 * w, :]
            vw = vp[row0:row0 + 2 * w, :]
            n = i % nb
            r = i // nb
            res = []
            for h, kp in ((0, kp0), (1, kp1)):
                kw = kp[row0:row0 + 2 * w, :]
                lg = _nt(qb, kw) + tab_ref[p, h]
                if n == 0:
                    lg = jnp.where(col >= w, lg, NEG)
                m = jnp.max(lg, axis=-1, keepdims=True)
                pe = jnp.exp(lg - m)
                l = jnp.sum(pe, axis=-1, keepdims=True)
                pv = _mm(pe.astype(BF16), vw)
                res.append((m, l, pv))
            mc = jnp.where(h0, res[0][0], res[1][0])
            lc = jnp.where(h0, res[0][1], res[1][1])
            ac = jnp.where(h0, res[0][2], res[1][2])
            tok = src_index(r + dil * w * n)
            mo = mrun[tok, :]
            mn = jnp.maximum(mo, mc)
            ea = jnp.exp(mo - mn)
            eb = jnp.exp(mc - mn)
            lrun[tok, :] = ea * lrun[tok, :] + eb * lc
            arun[tok, :] = ea * arun[tok, :] + eb * ac
            mrun[tok, :] = mn

    def out_body(c, carry):
        r0 = pl.multiple_of(c * rows, rows)
        o_ref[pl.ds(r0, rows), :] = (arun[pl.ds(r0, rows), :] / lrun[pl.ds(r0, rows), :]).astype(o_ref.dtype)
        return carry

    lax.fori_loop(0, s // rows, out_body, 0)


def _attention(qkv, q_gain, k_gain, tab, batch, seq):
    t = qkv.shape[0]
    npair = A_HEADS // 2
    qg = (jnp.tile(q_gain.astype(F32), 2) * HEAD_DIM ** -0.5).reshape(1, LANES)
    kg = jnp.tile(k_gain.astype(F32), 2).reshape(1, LANES)
    spad = seq + BAND
    return pl.pallas_call(
        _attn_kernel,
        grid=(batch, npair),
        in_specs=[
            pl.BlockSpec((seq, LANES), lambda b, h: (b, h)),
            pl.BlockSpec((seq, LANES), lambda b, h: (b, npair + h)),
            pl.BlockSpec((seq, LANES), lambda b, h: (b, 2 * npair + h)),
            pl.BlockSpec((1, LANES), lambda b, h: (0, 0)),
            pl.BlockSpec((1, LANES), lambda b, h: (0, 0)),
            pl.BlockSpec((len(A_PATTERNS), 2, BAND, 2 * BAND), lambda b, h: (0, h, 0, 0)),
        ],
        out_specs=pl.BlockSpec((seq, LANES), lambda b, h: (b, h)),
        out_shape=jax.ShapeDtypeStruct((t, A_WIDTH), BF16),
        scratch_shapes=[
            pltpu.VMEM((seq, LANES), F32),
            pltpu.VMEM((seq, LANES), F32),
            pltpu.VMEM((seq, LANES), F32),
            pltpu.VMEM((spad, LANES), BF16),
            pltpu.VMEM((spad, LANES), BF16),
            pltpu.VMEM((spad, LANES), BF16),
            pltpu.VMEM((spad, LANES), BF16),
            pltpu.VMEM((seq, LANES), F32),
            pltpu.VMEM((seq, LANES), F32),
            pltpu.VMEM((seq, LANES), F32),
        ],
        compiler_params=_params(("parallel", "parallel")),
        name="dilated_attn",
    )(qkv, qkv, qkv, qg, kg, tab)


B_SUB = 16
B_ROWS = 4 * B_CHUNK
B_NSUB = B_ROWS // B_SUB
B_PER = B_CHUNK // B_SUB
B_PAIRS = tuple((i, j) for i in range(B_NSUB) for j in range(i) if i // B_PER == j // B_PER)


def _hgrn_consts():
    c = B_CHUNK
    rows = B_ROWS
    hk = B_QK_WIDTH
    hv = B_WIDTH
    sub = B_SUB
    nsub = B_NSUB
    npair = len(B_PAIRS)
    tri = np.kron(np.eye(rows // c, dtype=np.float32), np.tril(np.ones((c, c), np.float32)))
    gt = np.zeros((B_PER * B_HEADS * sub, sub * hk), np.float32)
    bd = np.zeros((hv, hk), np.float32)
    hm = np.zeros((hv, hv), np.float32)
    hmk = np.zeros((B_HEADS, hk), np.float32)
    hmv = np.zeros((B_HEADS, hv), np.float32)
    md = np.zeros((rows, B_PER * B_HEADS * sub), np.float32)
    mo = np.zeros((rows, B_HEADS * npair * sub), np.float32)
    for h in range(B_HEADS):
        bd[h * B_VAL_DIM:(h + 1) * B_VAL_DIM, h * B_KEY_DIM:(h + 1) * B_KEY_DIM] = 1.0
        hm[h * B_VAL_DIM:(h + 1) * B_VAL_DIM, h * B_VAL_DIM:(h + 1) * B_VAL_DIM] = 1.0
        hmk[h, h * B_KEY_DIM:(h + 1) * B_KEY_DIM] = 1.0
        hmv[h, h * B_VAL_DIM:(h + 1) * B_VAL_DIM] = 1.0
        for i in range(B_PER):
            for s in range(sub):
                row = (i * B_HEADS + h) * sub + s
                gt[row, s * hk + h * B_KEY_DIM:s * hk + (h + 1) * B_KEY_DIM] = 1.0
        for i in range(nsub):
            i4 = i % B_PER
            md[i * sub:(i + 1) * sub, (i4 * B_HEADS + h) * sub:(i4 * B_HEADS + h + 1) * sub] = 1.0
        for p, (i, j) in enumerate(B_PAIRS):
            mo[i * sub:(i + 1) * sub, (h * npair + p) * sub:(h * npair + p + 1) * sub] = 1.0
    return tri, gt, bd, hm, hmk, hmv, md, mo


def _hgrn_kernel(qb_ref, fb_ref, ib_ref, og_ref, lbl_ref, onorm_ref, tri_ref, gt_ref, bd_ref, hm_ref,
                 hmk_ref, hmv_ref, md_ref, mo_ref, o_ref, st, dstack, bsc, qsc, ksc, osc, *, layer):
    s = qb_ref.shape[0]
    c = B_CHUNK
    rows = B_ROWS
    hk = B_QK_WIDTH
    sub = B_SUB
    nsub = B_NSUB

    st[...] = jnp.zeros(st.shape, F32)
    lbl = lbl_ref[...]
    e = jnp.exp(lbl - jnp.max(lbl, axis=0, keepdims=True))
    lb = jnp.sum(e[0:layer + 1, :], axis=0, keepdims=True) / jnp.sum(e, axis=0, keepdims=True)
    tri = tri_ref[...]

    def block(bi, carry):
        r0 = pl.multiple_of(bi * rows, rows)
        qr = qb_ref[pl.ds(r0, rows), :]
        q = qr * jax.nn.sigmoid(qr) * (B_KEY_DIM ** -0.5)
        f = lb + (1.0 - lb) * jax.nn.sigmoid(fb_ref[pl.ds(r0, rows), :])
        g1, g2, g3 = _split3(jnp.log(f))
        bcum = _mm(tri, g1) + _mm(tri, g2) + _mm(tri, g3)
        bsc[...] = bcum
        qsc[...] = q
        ksc[...] = 1.0 - f
        trow = lax.broadcasted_iota(I32, (rows, 1), 0)

        def vrows(lo, n):
            return ib_ref[pl.ds(r0 + lo, n), :]

        def rows_of(ref, off):
            return jnp.concatenate(
                [jnp.broadcast_to(ref[i * sub + off:i * sub + off + 1, :], (sub, ref.shape[1])) for i in range(nsub)],
                axis=0)

        rho = jnp.concatenate(
            [jnp.zeros((sub, hk), F32) if i % B_PER == 0
             else jnp.broadcast_to(bsc[i * sub - 1:i * sub, :], (sub, hk)) for i in range(nsub)], axis=0)
        qt = (qsc[...] * jnp.exp(bsc[...] - rho)).astype(BF16)
        kst = jnp.concatenate(
            [ksc[j * sub:(j + 1) * sub, :] * jnp.exp(bsc[i * sub - 1:i * sub, :] - bsc[j * sub:(j + 1) * sub, :])
             for i, j in B_PAIRS], axis=0)
        vst = jnp.concatenate([vrows(j * sub, sub) for _, j in B_PAIRS], axis=0)
        kall = jnp.concatenate([kst * hmk_ref[h:h + 1, :] for h in range(B_HEADS)], axis=0).astype(BF16)
        vall = jnp.concatenate([vst * hmv_ref[h:h + 1, :] for h in range(B_HEADS)], axis=0).astype(BF16)
        s_off = _nt(qt, kall) * mo_ref[...]
        osc[...] = _mm(s_off.astype(BF16), vall)

        for sp in range(sub):
            ok = (trow & (sub - 1)) >= sp
            ex = jnp.exp(jnp.where(ok, bsc[...] - rows_of(bsc, sp), NEG))
            dv = (qsc[...] * rows_of(ksc, sp)) * ex
            dstack[:, sp * hk:(sp + 1) * hk] = dv.astype(BF16)
        s_dia = (_nt(dstack[...], gt_ref[...]) * md_ref[...]).astype(BF16)

        for cc in range(rows // c):
            lo, hi = cc * c, (cc + 1) * c
            vdia = jnp.concatenate(
                [vrows(lo + i * sub, sub) * hmv_ref[h:h + 1, :] for i in range(B_PER) for h in range(B_HEADS)],
                axis=0).astype(BF16)
            stv = st[...]
            qe = (qsc[lo:hi, :] * jnp.exp(bsc[lo:hi, :])).astype(BF16)
            osc[lo:hi, :] += _mm(s_dia[lo:hi, :], vdia) + _nt(qe, stv.astype(BF16))
            blast = bsc[hi - 1:hi, :]
            kd = ksc[lo:hi, :] * jnp.exp(blast - bsc[lo:hi, :])
            st[...] = stv * jnp.exp(blast) + _tn(vrows(lo, c).astype(BF16), kd.astype(BF16)) * bd_ref[...]

        o = osc[...]
        o2 = o * o
        o2h = o2.astype(BF16)
        o2l = (o2 - o2h.astype(F32)).astype(BF16)
        ms = (_mm(o2h, hm_ref[...]) + _mm(o2l, hm_ref[...])) * (1.0 / B_VAL_DIM)
        ogr = og_ref[pl.ds(r0, rows), :]
        y = o * lax.rsqrt(ms + EPS) * onorm_ref[...] * (ogr * jax.nn.sigmoid(ogr))
        o_ref[pl.ds(r0, rows), :] = y.astype(o_ref.dtype)
        return carry

    lax.fori_loop(0, s // rows, block, 0)


def _hgrn(qf, io, lb_logits, layer, out_norm, batch, seq):
    t = qf.shape[0]
    tri, gt, bd, hm, hmk, hmv, md, mo = _hgrn_consts()
    nslot = lb_logits.shape[0]
    const = lambda a, dt: jnp.asarray(a, dt)
    full = lambda shp: pl.BlockSpec(shp, lambda b: (0,) * len(shp))
    return pl.pallas_call(
        functools.partial(_hgrn_kernel, layer=layer),
        grid=(batch,),
        in_specs=[
            pl.BlockSpec((seq, B_QK_WIDTH), lambda b: (b, 0)),
            pl.BlockSpec((seq, B_QK_WIDTH), lambda b: (b, 1)),
            pl.BlockSpec((seq, B_WIDTH), lambda b: (b, 0)),
            pl.BlockSpec((seq, B_WIDTH), lambda b: (b, 1)),
            full((nslot, B_QK_WIDTH)),
            full((1, B_WIDTH)),
            full(tri.shape), full(gt.shape), full(bd.shape), full(hm.shape),
            full(hmk.shape), full(hmv.shape), full(md.shape), full(mo.shape),
        ],
        out_specs=pl.BlockSpec((seq, B_WIDTH), lambda b: (b, 0)),
        out_shape=jax.ShapeDtypeStruct((t, B_WIDTH), BF16),
        scratch_shapes=[
            pltpu.VMEM((B_WIDTH, B_QK_WIDTH), F32),
            pltpu.VMEM((B_ROWS, B_SUB * B_QK_WIDTH), BF16),
            pltpu.VMEM((B_ROWS, B_QK_WIDTH), F32),
            pltpu.VMEM((B_ROWS, B_QK_WIDTH), F32),
            pltpu.VMEM((B_ROWS, B_QK_WIDTH), F32),
            pltpu.VMEM((B_ROWS, B_WIDTH), F32),
        ],
        compiler_params=_params(("parallel",)),
        name="hgrn2",
    )(qf, qf, io, io, lb_logits.astype(F32), out_norm.astype(F32).reshape(1, B_WIDTH),
      const(tri, BF16), const(gt, BF16), const(bd, F32), const(hm, BF16),
      const(hmk, F32), const(hmv, F32), const(md, F32), const(mo, F32))


def _head_ms(t, masks):
    sq = t * t
    ms = jnp.zeros_like(t)
    for mk in masks:
        ms = jnp.where(mk, jnp.sum(jnp.where(mk, sq, 0.0), axis=-1, keepdims=True), ms)
    return ms * (1.0 / HEAD_DIM)


def _mem_kernel(mem_ref, mg_ref, wkv_ref, q_ref, qg_ref, kg_ref, o_ref, ksc, vsc):
    s = q_ref.shape[0]
    lane = lax.broadcasted_iota(I32, (1, C_WIDTH), 1)
    masks = [(lane >= h * HEAD_DIM) & (lane < (h + 1) * HEAD_DIM) for h in range(C_HEADS)]

    m = mem_ref[0]
    ms = jnp.mean(m * m, axis=-1, keepdims=True)
    mn = (m * lax.rsqrt(ms + EPS) * mg_ref[...]).astype(BF16)
    kv = _mm(mn, wkv_ref[...])
    km = kv[:, :C_WIDTH]
    k = km * lax.rsqrt(_head_ms(km, masks) + EPS) * kg_ref[...]
    for h in range(C_HEADS):
        ksc[h] = jnp.where(masks[h], k, 0.0).astype(BF16)
    vsc[...] = kv[:, C_WIDTH:].astype(BF16)

    rows = 256

    def body(c, carry):
        r0 = pl.multiple_of(c * rows, rows)
        q = q_ref[pl.ds(r0, rows), :].astype(F32)
        qn = (q * lax.rsqrt(_head_ms(q, masks) + EPS) * qg_ref[...]).astype(BF16)
        out = jnp.zeros((rows, C_WIDTH), F32)
        for h in range(C_HEADS):
            lg = _nt(qn, ksc[h])
            mx = jnp.max(lg, axis=-1, keepdims=True)
            pe = jnp.exp(lg - mx)
            l = jnp.sum(pe, axis=-1, keepdims=True)
            out = jnp.where(masks[h], _mm(pe.astype(BF16), vsc[...]) / l, out)
        o_ref[pl.ds(r0, rows), :] = out.astype(o_ref.dtype)
        return carry

    lax.fori_loop(0, s // rows, body, 0)


def _memory(mem, mem_g, wkv_bf16, qc, q_gain, k_gain, seq):
    batch, mlen, d = mem.shape
    t = qc.shape[0]
    qg = (jnp.tile(q_gain.astype(F32), C_HEADS) * HEAD_DIM ** -0.5).reshape(1, C_WIDTH)
    kg = jnp.tile(k_gain.astype(F32), C_HEADS).reshape(1, C_WIDTH)
    return pl.pallas_call(
        _mem_kernel,
        grid=(batch,),
        in_specs=[
            pl.BlockSpec((1, mlen, d), lambda b: (b, 0, 0)),
            pl.BlockSpec((1, d), lambda b: (0, 0)),
            pl.BlockSpec(wkv_bf16.shape, lambda b: (0, 0)),
            pl.BlockSpec((seq, C_WIDTH), lambda b: (b, 0)),
            pl.BlockSpec((1, C_WIDTH), lambda b: (0, 0)),
            pl.BlockSpec((1, C_WIDTH), lambda b: (0, 0)),
        ],
        out_specs=pl.BlockSpec((seq, C_WIDTH), lambda b: (b, 0)),
        out_shape=jax.ShapeDtypeStruct((t, C_WIDTH), BF16),
        scratch_shapes=[
            pltpu.VMEM((C_HEADS, mlen, C_WIDTH), BF16),
            pltpu.VMEM((mlen, C_WIDTH), BF16),
        ],
        compiler_params=_params(("parallel",)),
        name="mem_xattn",
    )(mem, mem_g.astype(F32).reshape(1, d), wkv_bf16, qc, qg, kg)


def _merge_kernel(x_ref, ya_ref, yb_ref, yc_ref, gate_ref, wa_ref, wb_ref, wc_ref, wo_ref, fg_ref,
                  wrh_ref, wrl_ref, wsu_ref, wsd_ref, x2_ref, up_ref, st_ref):
    d = x_ref.shape[1]
    mixed = jnp.zeros(x_ref.shape, F32)
    for i, (y_ref, w_ref) in enumerate(((ya_ref, wa_ref), (yb_ref, wb_ref), (yc_ref, wc_ref))):
        g = jax.nn.sigmoid(gate_ref[:, i * d:(i + 1) * d].astype(F32))
        mixed = mixed + g * _mm(y_ref[...], w_ref[...])
    x1 = x_ref[...] + _mm(mixed.astype(BF16), wo_ref[...])
    ms = jnp.mean(x1 * x1, axis=-1, keepdims=True)
    u = x1 * lax.rsqrt(ms + EPS) * fg_ref[...]
    ub = u.astype(BF16)
    ul = (u - ub.astype(F32)).astype(BF16)
    logits = _nt(wrh_ref[...], ub) + _nt(wrl_ref[...], ub) + _nt(wrh_ref[...], ul)
    st_ref[...] = jax.nn.sigmoid(logits)
    hid = _mm(ub, wsu_ref[...])
    nh = hid.shape[1] // 2
    act = (jax.nn.silu(hid[:, :nh]) * hid[:, nh:]).astype(BF16)
    x2_ref[...] = x1 + _mm(act, wsd_ref[...])
    up_ref[...] = ub


def _merge(x2d, ya, yb, yc, gates, wa, wb, wc, wo, fg, wr_hi, wr_lo, wsu, wsd):
    t, d = x2d.shape
    tm = TM_PROJ
    row = lambda w: pl.BlockSpec((tm, w), lambda i: (i, 0))
    full = lambda a: pl.BlockSpec(a.shape, lambda i: (0,) * a.ndim)
    fg2 = fg.astype(F32).reshape(1, d)
    return pl.pallas_call(
        _merge_kernel,
        grid=(t // tm,),
        in_specs=[row(d), row(A_WIDTH), row(B_WIDTH), row(C_WIDTH), row(3 * d),
                  full(wa), full(wb), full(wc), full(wo), full(fg2), full(wr_hi), full(wr_lo),
                  full(wsu), full(wsd)],
        out_specs=[row(d), row(d), pl.BlockSpec((N_EXPERTS, tm), lambda i: (0, i))],
        out_shape=[jax.ShapeDtypeStruct((t, d), F32),
                   jax.ShapeDtypeStruct((t, d), BF16),
                   jax.ShapeDtypeStruct((N_EXPERTS, t), F32)],
        compiler_params=_params(("parallel",)),
        name="merge_router_shared",
    )(x2d, ya, yb, yc, gates, wa, wb, wc, wo, fg2, wr_hi, wr_lo, wsu, wsd)


def _row_to_expert():
    per_group = N_EXPERTS // N_GROUPS
    rows = np.arange(N_EXPERTS)
    return (rows % N_GROUPS) * per_group + rows // N_GROUPS


def _route_kernel(s_ref, bias_ref, eidx_ref, ut_ref, ones_ref, ls_ref,
                  idx_ref, w_ref, lpos_ref, seg_ref, loff_ref, car_ref, carry):
    tr = s_ref.shape[1]
    per_group = N_EXPERTS // N_GROUPS
    ng = N_GROUPS
    nrep = tr // LANES
    ninf = -jnp.inf

    @pl.when(pl.program_id(0) == 0)
    def _():
        carry[...] = jnp.zeros(carry.shape, F32)

    s_all = s_ref[...]
    bias = jnp.concatenate([bias_ref[...]] * nrep, axis=1)
    eidx = jnp.concatenate([eidx_ref[...]] * nrep, axis=1)
    sv = [s_all[i * ng:(i + 1) * ng, :] for i in range(per_group)]
    ev = [eidx[i * ng:(i + 1) * ng, :] for i in range(per_group)]
    cv = [sv[i] + bias[i * ng:(i + 1) * ng, :] for i in range(per_group)]

    def vmax(xs):
        out = xs[0]
        for x in xs[1:]:
            out = jnp.maximum(out, x)
        return out

    def vmin(xs):
        out = xs[0]
        for x in xs[1:]:
            out = jnp.minimum(out, x)
        return out

    m1 = vmax(cv)
    i1 = vmin([jnp.where(cv[i] == m1, i, per_group) for i in range(per_group)])
    m2 = vmax([jnp.where(i1 == i, ninf, cv[i]) for i in range(per_group)])
    gs = m1 + m2
    giota = lax.broadcasted_iota(I32, (ng, tr), 0)
    gmask = jnp.zeros((ng, tr), jnp.bool_)
    for _ in range(TOPK_GROUPS):
        mx = jnp.max(gs, axis=0, keepdims=True)
        pick = jnp.min(jnp.where(gs == mx, giota, ng), axis=0, keepdims=True)
        sel = giota == pick
        gmask = jnp.logical_or(gmask, sel)
        gs = jnp.where(sel, ninf, gs)

    cm = [jnp.where(gmask, cv[i], ninf) for i in range(per_group)]
    idx_out = jnp.zeros((TOP_K, tr), I32)
    w_out = jnp.zeros((TOP_K, tr), F32)
    jiota = lax.broadcasted_iota(I32, (TOP_K, tr), 0)
    for j in range(TOP_K):
        mx = jnp.max(vmax(cm), axis=0, keepdims=True)
        emin = jnp.min(vmin([jnp.where(cm[i] == mx, ev[i], N_EXPERTS) for i in range(per_group)]),
                       axis=0, keepdims=True)
        hit = [ev[i] == emin for i in range(per_group)]
        ssel = hit_sum = None
        for i in range(per_group):
            term = jnp.where(hit[i], sv[i], 0.0)
            hit_sum = term if hit_sum is None else hit_sum + term
        ssel = jnp.sum(hit_sum, axis=0, keepdims=True)
        cm = [jnp.where(hit[i], ninf, cm[i]) for i in range(per_group)]
        idx_out = jnp.where(jiota == j, emin, idx_out)
        w_out = jnp.where(jiota == j, ssel, w_out)
    idx_ref[...] = idx_out
    w_ref[...] = ROUTED_SCALE * w_out / jnp.sum(w_out, axis=0, keepdims=True)

    chosen = [jnp.where(jnp.logical_and(gmask, cm[i] == ninf), 1.0, 0.0) for i in range(per_group)]
    mh = jnp.concatenate(chosen, axis=0).astype(BF16)
    cnt = _mm(mh, ones_ref[...])
    seg8 = ((cnt.astype(I32) + (SUBLANES - 1)) // SUBLANES) * SUBLANES
    seg8f = seg8.astype(F32)
    loff = _mm(ls_ref[...], seg8f.astype(BF16))
    lfull = _mm(mh, ut_ref[...]) + jnp.concatenate([loff] * nrep, axis=1)
    lv = [lfull[i * ng:(i + 1) * ng, :] for i in range(per_group)]
    lpos = jnp.zeros((TOP_K, tr), F32)
    for j in range(TOP_K):
        ej = idx_out[j:j + 1, :]
        acc = None
        for i in range(per_group):
            term = jnp.where(ev[i] == ej, lv[i], 0.0)
            acc = term if acc is None else acc + term
        lpos = jnp.where(jiota == j, jnp.sum(acc, axis=0, keepdims=True), lpos)
    lpos_ref[...] = lpos.astype(I32)
    cr = carry[...]
    seg_ref[0] = seg8
    loff_ref[0] = loff.astype(I32)
    car_ref[0] = cr.astype(I32)
    carry[...] = cr + seg8f


def _route(s_t, bias_rows, t):
    tr = TB_MOE
    nt = t // tr
    r2e = _row_to_expert()
    eidx = jnp.asarray(np.broadcast_to(r2e[:, None], (N_EXPERTS, LANES)).astype(np.int32))
    bias = jnp.broadcast_to(bias_rows.astype(F32)[:, None], (N_EXPERTS, LANES))
    ut = jnp.asarray(np.triu(np.ones((tr, tr), np.float32), k=1), BF16)
    ls = jnp.asarray(np.tril(np.ones((N_EXPERTS, N_EXPERTS), np.float32), k=-1), BF16)
    ones = jnp.ones((tr, LANES), BF16)
    full = lambda a: pl.BlockSpec(a.shape, lambda i: (0,) * a.ndim)
    tile = pl.BlockSpec((TOP_K, tr), lambda i: (0, i))
    tab = pl.BlockSpec((1, N_EXPERTS, LANES), lambda i: (i, 0, 0))
    tab_shape = jax.ShapeDtypeStruct((nt, N_EXPERTS, LANES), I32)
    return pl.pallas_call(
        _route_kernel,
        grid=(nt,),
        in_specs=[pl.BlockSpec((N_EXPERTS, tr), lambda i: (0, i)), full(bias), full(eidx), full(ut), full(ones),
                  full(ls)],
        out_specs=[tile, tile, tile, tab, tab, tab],
        out_shape=[jax.ShapeDtypeStruct((TOP_K, t), I32),
                   jax.ShapeDtypeStruct((TOP_K, t), F32),
                   jax.ShapeDtypeStruct((TOP_K, t), I32),
                   tab_shape, tab_shape, tab_shape],
        scratch_shapes=[pltpu.VMEM((N_EXPERTS, LANES), F32)],
        compiler_params=_params(("arbitrary",)),
        name="route_topk_layout",
    )(s_t, bias, eidx, ut, ones, ls)


def _piece_copies(grow_ref, used_ref, tile, local_ref, hbm_ref, sem, to_hbm):
    base = tile * PIECES_MOE

    def piece(p, carry):
        loc = local_ref.at[pl.ds(pl.multiple_of(p * SUBLANES, SUBLANES), SUBLANES)]
        glb = hbm_ref.at[pl.ds(pl.multiple_of(grow_ref[base + p], SUBLANES), SUBLANES)]
        if to_hbm:
            pltpu.make_async_copy(loc, glb, sem).start()
        else:
            pltpu.make_async_copy(glb, loc, sem).start()
        return carry

    lax.fori_loop(0, used_ref[tile], piece, 0)


def _wait_rows(n8, local_ref, hbm_ref, sem):
    for bit in USED_BITS:
        @pl.when((n8 & bit) != 0)
        def _():
            size = SUBLANES * bit
            pltpu.make_async_copy(local_ref.at[pl.ds(0, size)], hbm_ref.at[pl.ds(0, size)], sem).wait()


def _dispatch_kernel(grow_ref, used_ref, zrow_ref, nused_ref, lpos_ref, u_ref, xs_hbm,
                     xloc, zbuf, sems, zsem):
    i = pl.program_id(0)
    nt = pl.num_programs(0)
    tb = u_ref.shape[0]
    bm = zbuf.shape[0]
    slot = lax.rem(i, 2)

    def zero_copy(e):
        return pltpu.make_async_copy(zbuf, xs_hbm.at[pl.ds(pl.multiple_of(zrow_ref[e], bm), bm)], zsem)

    @pl.when(i == 0)
    def _():
        zbuf[...] = jnp.zeros(zbuf.shape, zbuf.dtype)

        def zstart(e, carry):
            @pl.when(zrow_ref[e] >= 0)
            def _():
                zero_copy(e).start()
            return carry

        def zwait(e, carry):
            @pl.when(zrow_ref[e] >= 0)
            def _():
                zero_copy(e).wait()
            return carry

        lax.fori_loop(0, N_EXPERTS, zstart, 0)
        lax.fori_loop(0, N_EXPERTS, zwait, 0)

        def tail_copy(b):
            return pltpu.make_async_copy(zbuf, xs_hbm.at[pl.ds(pl.multiple_of(b * bm, bm), bm)], zsem)

        def tstart(b, carry):
            tail_copy(b).start()
            return carry

        def twait(b, carry):
            tail_copy(b).wait()
            return carry

        nblk = xs_hbm.shape[0] // bm
        lax.fori_loop(nused_ref[0], nblk, tstart, 0)
        lax.fori_loop(nused_ref[0], nblk, twait, 0)

    @pl.when(i >= 2)
    def _():
        _wait_rows(used_ref[i - 2], xloc.at[slot], xs_hbm, sems.at[slot])

    used = used_ref[i]
    ub = u_ref[...]
    for rc in range(RL_MOE // RC_MOE):
        @pl.when(rc * (RC_MOE // SUBLANES) < used)
        def _():
            rows = lax.broadcasted_iota(I32, (RC_MOE, tb), 0) + rc * RC_MOE
            onehot = jnp.zeros((RC_MOE, tb), F32)
            for j in range(TOP_K):
                onehot = jnp.where(rows == lpos_ref[j:j + 1, :], 1.0, onehot)
            xloc[slot, rc * RC_MOE:(rc + 1) * RC_MOE, :] = _pack_pairs(_mm(onehot.astype(BF16), ub))

    _piece_copies(grow_ref, used_ref, i, xloc.at[slot], xs_hbm, sems.at[slot], True)

    @pl.when(i == nt - 1)
    def _():
        _wait_rows(used, xloc.at[slot], xs_hbm, sems.at[slot])

        @pl.when(i >= 1)
        def _():
            _wait_rows(used_ref[i - 1], xloc.at[1 - slot], xs_hbm, sems.at[1 - slot])


def _dispatch(tables, zrow, n_used, lpos_t, u_bf16, n_rows):
    t, d = u_bf16.shape
    tb = TB_MOE
    half = d // 2
    nsp = len(tables) + 2
    return pl.pallas_call(
        _dispatch_kernel,
        grid_spec=pltpu.PrefetchScalarGridSpec(
            num_scalar_prefetch=nsp,
            grid=(t // tb,),
            in_specs=[pl.BlockSpec((TOP_K, tb), lambda i, *_: (0, i)),
                      pl.BlockSpec((tb, d), lambda i, *_: (i, 0))],
            out_specs=pl.BlockSpec(memory_space=pl.ANY),
            scratch_shapes=[
                pltpu.VMEM((2, RL_MOE, half), U32),
                pltpu.VMEM((BM_EXPERT, half), U32),
                pltpu.SemaphoreType.DMA((2,)),
                pltpu.SemaphoreType.DMA,
            ],
        ),
        out_shape=jax.ShapeDtypeStruct((n_rows, half), U32),
        compiler_params=_params(("arbitrary",)),
        name="dispatch_rows",
    )(*tables, zrow, n_used, lpos_t, u_bf16)


def _expert_kernel(be_ref, nu_ref, xs_ref, wu_ref, wd_ref, ys_ref):
    @pl.when(pl.program_id(0) < nu_ref[0])
    def _():
        xa, xb = _unpack_pairs(xs_ref[...])
        half = xa.shape[1]
        hid = _mm(xa.astype(BF16), wu_ref[0, :half, :]) + _mm(xb.astype(BF16), wu_ref[0, half:, :])
        nh = hid.shape[1] // 2
        act = (jax.nn.silu(hid[:, :nh]) * hid[:, nh:]).astype(BF16)
        ys_ref[...] = _pack_pairs(_mm(act, wd_ref[0]))

    @pl.when(pl.program_id(0) >= nu_ref[0])
    def _():
        ys_ref[...] = jnp.zeros(ys_ref.shape, ys_ref.dtype)


def _experts(block_expert, n_used, xs, wu, wd):
    n_rows, half = xs.shape
    bm = BM_EXPERT
    nblk = n_rows // bm
    d = 2 * half
    last = lambda b, nu: jnp.minimum(b, nu[0] - 1)
    return pl.pallas_call(
        _expert_kernel,
        grid_spec=pltpu.PrefetchScalarGridSpec(
            num_scalar_prefetch=2,
            grid=(nblk,),
            in_specs=[
                pl.BlockSpec((bm, half), lambda b, be, nu: (last(b, nu), 0)),
                pl.BlockSpec((1, d, wu.shape[2]), lambda b, be, nu: (be[last(b, nu)], 0, 0)),
                pl.BlockSpec((1, wd.shape[1], d), lambda b, be, nu: (be[last(b, nu)], 0, 0)),
            ],
            out_specs=pl.BlockSpec((bm, half), lambda b, be, nu: (b, 0)),
        ),
        out_shape=jax.ShapeDtypeStruct((n_rows, half), U32),
        compiler_params=_params(("arbitrary",)),
        name="expert_swiglu",
    )(block_expert, n_used, xs, wu, wd)


def _combine_kernel(grow_ref, used_ref, ys_hbm, lpos_ref, w_ref, x2_ref, o_ref,
                    yloc, acc, sems):
    i = pl.program_id(0)
    nt = pl.num_programs(0)
    tb = x2_ref.shape[0]
    half = yloc.shape[2]
    slot = lax.rem(i, 2)

    def fetch(tile, sl):
        _piece_copies(grow_ref, used_ref, tile, yloc.at[sl], ys_hbm, sems.at[sl], False)

    @pl.when(i == 0)
    def _():
        yloc[...] = jnp.zeros(yloc.shape, yloc.dtype)
        fetch(0, 0)

    @pl.when(i + 1 < nt)
    def _():
        fetch(i + 1, 1 - slot)

    used = used_ref[i]
    _wait_rows(used, yloc.at[slot], ys_hbm, sems.at[slot])

    acc[...] = x2_ref[...]
    for kc in range(RL_MOE // RC_MOE):
        @pl.when(kc * (RC_MOE // SUBLANES) < used)
        def _():
            rows = lax.broadcasted_iota(I32, (RC_MOE, tb), 0) + kc * RC_MOE
            pw = jnp.zeros((RC_MOE, tb), F32)
            for j in range(TOP_K):
                pw = jnp.where(rows == lpos_ref[j:j + 1, :], w_ref[j:j + 1, :], pw)
            pw = pw.astype(BF16)
            ya, yb = _unpack_pairs(yloc[slot, kc * RC_MOE:(kc + 1) * RC_MOE, :])
            acc[:, :half] += _tn(pw, ya.astype(BF16))
            acc[:, half:] += _tn(pw, yb.astype(BF16))
    o_ref[...] = acc[...]


def _combine(tables, ys, lpos_t, w_t, x2):
    t, d = x2.shape
    tb = TB_MOE
    half = d // 2
    nsp = len(tables)
    tok = lambda w: pl.BlockSpec((tb, w), lambda i, *_: (i, 0))
    slots = pl.BlockSpec((TOP_K, tb), lambda i, *_: (0, i))
    return pl.pallas_call(
        _combine_kernel,
        grid_spec=pltpu.PrefetchScalarGridSpec(
            num_scalar_prefetch=nsp,
            grid=(t // tb,),
            in_specs=[pl.BlockSpec(memory_space=pl.ANY), slots, slots, tok(d)],
            out_specs=tok(d),
            scratch_shapes=[
                pltpu.VMEM((2, RL_MOE, half), U32),
                pltpu.VMEM((tb, d), F32),
                pltpu.SemaphoreType.DMA((2,)),
            ],
        ),
        out_shape=jax.ShapeDtypeStruct((t, d), F32),
        compiler_params=_params(("arbitrary",)),
        name="combine_rows",
    )(*tables, ys, lpos_t, w_t, x2)


def kernel(x, mem, attn_norm_g, w_in, q_norm_a, k_norm_a, rel_bias, lb_logits, out_norm_b, mem_norm_g, w_mem_kv,
           q_norm_c, k_norm_c, w_branch_a, w_branch_b, w_branch_c, w_out, ffn_norm_g, w_router, router_bias,
           w_exp_up, w_exp_down, w_shared_up, w_shared_down):
    batch, seq, d = x.shape
    t = batch * seq
    depth = w_in.shape[0]
    tab = _bias_tables(rel_bias)
    r2e = _row_to_expert()
    xf = x.reshape(t, d)
    for layer in range(depth):
        qkv, qf, io, qc, gates = _inproj(xf, attn_norm_g[layer], w_in[layer].astype(BF16))
        ya = _attention(qkv, q_norm_a[layer], k_norm_a[layer], tab, batch, seq)
        yb = _hgrn(qf, io, lb_logits, layer, out_norm_b[layer], batch, seq)
        yc = _memory(mem, mem_norm_g[layer], w_mem_kv[layer].astype(BF16), qc, q_norm_c[layer], k_norm_c[layer], seq)

        wr_rows = w_router[layer].astype(F32).T[r2e]
        wr_hi = wr_rows.astype(BF16)
        wr_lo = (wr_rows - wr_hi.astype(F32)).astype(BF16)
        x2, u_bf16, s_t = _merge(
            xf, ya, yb, yc, gates,
            w_branch_a[layer].astype(BF16), w_branch_b[layer].astype(BF16), w_branch_c[layer].astype(BF16),
            w_out[layer].astype(BF16), ffn_norm_g[layer], wr_hi, wr_lo,
            w_shared_up[layer].astype(BF16), w_shared_down[layer].astype(BF16))

        _, w_t, lpos_t, seg_tab, loff_tab, car_tab = _route(s_t, router_bias[layer][r2e], t)

        bm = BM_EXPERT
        nt = t // TB_MOE
        seg = seg_tab[:, :, 0]
        loff = loff_tab[:, :, 0]
        car = car_tab[:, :, 0]
        total = car[-1] + seg[-1]
        padded = (total + bm - 1) // bm * bm
        pend = jnp.cumsum(padded)
        pstart = pend - padded
        n_rows = (t * TOP_K + N_EXPERTS * (SUBLANES - 1) * nt) // bm * bm + N_EXPERTS * bm
        nblk = n_rows // bm
        block_row = jnp.arange(nblk, dtype=I32) * bm
        block_slot = jnp.minimum(jnp.sum((pend[None, :] <= block_row[:, None]).astype(I32), axis=1), N_EXPERTS - 1)
        block_expert = jnp.asarray(r2e, I32)[block_slot]
        n_used = (pend[-1:] // bm).astype(I32)
        zrow = jnp.where(padded > total, pend - bm, -1).astype(I32)
        seg_end = (loff + seg) // SUBLANES
        piece = jnp.arange(PIECES_MOE, dtype=I32)
        owner = jnp.minimum(jnp.sum((seg_end[:, None, :] <= piece[None, :, None]).astype(I32), axis=2),
                            N_EXPERTS - 1)
        shift = pstart[None, :] + car - loff
        is_owner = owner[:, :, None] == jnp.arange(N_EXPERTS, dtype=I32)[None, None, :]
        grow = jnp.sum(jnp.where(is_owner, shift[:, None, :], 0), axis=2) + piece[None, :] * SUBLANES
        tables = (grow.reshape(-1).astype(I32), seg_end[:, -1].astype(I32))

        xs = _dispatch(tables, zrow, n_used, lpos_t, u_bf16, n_rows)
        ys = _experts(block_expert, n_used, xs, w_exp_up[layer].astype(BF16), w_exp_down[layer].astype(BF16))
        xf = _combine(tables, ys, lpos_t, w_t, x2)
    return xf.reshape(batch, seq, d)
```

```python
import functools
import math

import numpy as np
import jax
import jax.numpy as jnp
from jax import lax
from jax.experimental import pallas as pl
from jax.experimental.pallas import tpu as pltpu

F32 = jnp.float32
BF16 = jnp.bfloat16
I32 = jnp.int32
U32 = jnp.uint32

HEAD_DIM = 64
EPS = 1e-6
A_HEADS = 6
A_PATTERNS = ((128, 1), (512, 4), (2048, 16))
BAND = 128
N_BUCKETS = 32
REL_MAX_DIST = 2048
B_HEADS = 4
B_KEY_DIM = 64
B_VAL_DIM = 96
B_CHUNK = 64
C_HEADS = 4
N_EXPERTS = 64
TOP_K = 8
N_GROUPS = 8
TOPK_GROUPS = 4
EXPERT_HIDDEN = 256
ROUTED_SCALE = 2.5

A_WIDTH = A_HEADS * HEAD_DIM
B_QK_WIDTH = B_HEADS * B_KEY_DIM
B_WIDTH = B_HEADS * B_VAL_DIM
C_WIDTH = C_HEADS * HEAD_DIM

NEG = -1e30
HI_MASK = 0xFFFF0000

LANES = 128
SUBLANES = 8
VMEM_LIMIT = 56 * 1024 * 1024

TM_PROJ = 512
TB_MOE = 256
BM_EXPERT = 1024
RC_MOE = 256
RL_MOE = -(-(TB_MOE * TOP_K + N_EXPERTS * (SUBLANES - 1)) // RC_MOE) * RC_MOE
PIECES_MOE = RL_MOE // SUBLANES
USED_BITS = (256, 128, 64, 32, 16, 8, 4, 2, 1)


def _params(sem, vmem=VMEM_LIMIT):
    return pltpu.CompilerParams(dimension_semantics=sem, vmem_limit_bytes=vmem)


def _nt(a, b):
    return lax.dot_general(a, b, (((1,), (1,)), ((), ())), preferred_element_type=F32)


def _tn(a, b):
    return lax.dot_general(a, b, (((0,), (0,)), ((), ())), preferred_element_type=F32)


def _mm(a, b):
    return jnp.dot(a, b, preferred_element_type=F32)


def _split3(t):
    t1 = t.astype(BF16)
    r1 = t - t1.astype(F32)
    t2 = r1.astype(BF16)
    t3 = (r1 - t2.astype(F32)).astype(BF16)
    return t1, t2, t3


def _pack_pairs(y, already_bf16=False):
    n = y.shape[1] // 2
    bits = pltpu.bitcast(y if already_bf16 else y.astype(BF16).astype(F32), U32)
    return (bits[:, :n] & U32(HI_MASK)) | (bits[:, n:] >> U32(16))


def _unpack_pairs(p):
    hi = pltpu.bitcast(p & U32(HI_MASK), F32)
    lo = pltpu.bitcast(p << U32(16), F32)
    return hi, lo


def _inproj_kernel(x_ref, g_ref, w_ref, qkv_ref, qf_ref, io_ref, qc_ref, gate_ref):
    x = x_ref[...]
    ms = jnp.mean(x * x, axis=-1, keepdims=True)
    h = (x * lax.rsqrt(ms + EPS) * g_ref[...]).astype(BF16)
    col = 0
    for out in (qkv_ref, qf_ref, io_ref, qc_ref, gate_ref):
        width = out.shape[1]
        for a in range(0, width, 512):
            b = min(a + 512, width)
            out[:, a:b] = _mm(h, w_ref[:, col + a:col + b]).astype(out.dtype)
        col += width


def _inproj(x2d, g, w_bf16):
    t, d = x2d.shape
    widths = (3 * A_WIDTH, 2 * B_QK_WIDTH, 2 * B_WIDTH, C_WIDTH, 3 * d)
    dtypes = (BF16, F32, F32, BF16, BF16)
    tm = TM_PROJ
    return pl.pallas_call(
        _inproj_kernel,
        grid=(t // tm,),
        in_specs=[
            pl.BlockSpec((tm, d), lambda i: (i, 0)),
            pl.BlockSpec((1, d), lambda i: (0, 0)),
            pl.BlockSpec(w_bf16.shape, lambda i: (0, 0), pipeline_mode=pl.Buffered(1)),
        ],
        out_specs=[pl.BlockSpec((tm, w), lambda i: (i, 0)) for w in widths],
        out_shape=[jax.ShapeDtypeStruct((t, w), dt) for w, dt in zip(widths, dtypes)],
        compiler_params=_params(("parallel",)),
        name="inproj",
    )(x2d, g.reshape(1, d), w_bf16)


def _bucket_tables():
    w = BAND
    i = np.arange(w)[:, None]
    c = np.arange(2 * w)[None, :]
    steps = w + i - c
    in_band = (steps >= 0) & (steps <= w)
    max_exact = N_BUCKETS // 2
    tabs = []
    for _, dil in A_PATTERNS:
        d = (np.clip(steps, 0, w) * dil).astype(np.int32)
        ratio = np.maximum(d, 1).astype(np.float32) / np.float32(max_exact)
        large = max_exact + (np.log(ratio) / np.float32(math.log(REL_MAX_DIST / max_exact))
                             * np.float32(N_BUCKETS - max_exact)).astype(np.int32)
        bucket = np.where(d < max_exact, d, np.minimum(large, N_BUCKETS - 1))
        tabs.append(np.where(in_band, bucket, -1))
    return np.stack(tabs).astype(np.int32)


def _bias_kernel(rb_ref, bucket_ref, out_ref):
    h = pl.program_id(1)
    bk = bucket_ref[0]
    acc = jnp.full(bk.shape, NEG, F32)
    for b in range(N_BUCKETS):
        acc = jnp.where(bk == b, rb_ref[b, h], acc)
    out_ref[0, 0] = acc


def _bias_tables(rel_bias):
    buckets = jnp.asarray(_bucket_tables())
    p = len(A_PATTERNS)
    return pl.pallas_call(
        _bias_kernel,
        grid=(p, A_HEADS),
        in_specs=[
            pl.BlockSpec(memory_space=pltpu.SMEM),
            pl.BlockSpec((1, BAND, 2 * BAND), lambda i, h: (i, 0, 0)),
        ],
        out_specs=pl.BlockSpec((1, 1, BAND, 2 * BAND), lambda i, h: (i, h, 0, 0)),
        out_shape=jax.ShapeDtypeStruct((p, A_HEADS, BAND, 2 * BAND), F32),
        compiler_params=_params(("arbitrary", "arbitrary")),
        name="attn_bias",
    )(rel_bias.astype(F32), buckets)


def _attn_kernel(q_ref, k_ref, v_ref, qg_ref, kg_ref, tab_ref, o_ref,
                 qn, kn, vf, qp, kp0, kp1, vp, mrun, lrun, arun):
    s = q_ref.shape[0]
    w = BAND
    lane = lax.broadcasted_iota(I32, (1, LANES), 1)
    h0 = lane < HEAD_DIM
    rows = 256

    def norm_body(c, carry):
        r0 = pl.multiple_of(c * rows, rows)
        for src, dst, g in ((q_ref, qn, qg_ref), (k_ref, kn, kg_ref)):
            t = src[pl.ds(r0, rows), :].astype(F32)
            sq = t * t
            s0 = jnp.sum(jnp.where(h0, sq, 0.0), axis=-1, keepdims=True)
            s1 = jnp.sum(jnp.where(h0, 0.0, sq), axis=-1, keepdims=True)
            ms = jnp.where(h0, s0, s1) * (1.0 / HEAD_DIM)
            dst[pl.ds(r0, rows), :] = t * lax.rsqrt(ms + EPS) * g[...]
        vf[pl.ds(r0, rows), :] = v_ref[pl.ds(r0, rows), :].astype(F32)
        return carry

    lax.fori_loop(0, s // rows, norm_body, 0)

    mrun[...] = jnp.full(mrun.shape, NEG, F32)
    lrun[...] = jnp.zeros(lrun.shape, F32)
    arun[...] = jnp.zeros(arun.shape, F32)
    zpad = jnp.zeros((w, LANES), BF16)
    kp0[0:w, :] = zpad
    kp1[0:w, :] = zpad
    vp[0:w, :] = zpad

    col = lax.broadcasted_iota(I32, (1, 2 * w), 1)

    for p, (window, dil) in enumerate(A_PATTERNS):
        seg = s // dil
        nb = seg // w
        nblk = s // w

        def src_index(start):
            if dil == 1:
                return pl.ds(start, w)
            return pl.ds(start, w, stride=dil)

        for r in range(dil):
            for c in range(nb):
                idx = src_index(r + dil * w * c)
                dst = w + r * seg + c * w
                kv = kn[idx, :]
                qp[dst:dst + w, :] = qn[idx, :].astype(BF16)
                kp0[dst:dst + w, :] = jnp.where(h0, kv, 0.0).astype(BF16)
                kp1[dst:dst + w, :] = jnp.where(h0, 0.0, kv).astype(BF16)
                vp[dst:dst + w, :] = vf[idx, :].astype(BF16)

        for i in range(nblk):
            row0 = i * w
            qb = qp[row0 + w:row0 + 2 * w, :]
            vw = vp[row0:row0 + 2 * w, :]
            n = i % nb
            r = i // nb
            res = []
            for h, kp in ((0, kp0), (1, kp1)):
                kw = kp[row0:row0 + 2 * w, :]
                lg = _nt(qb, kw) + tab_ref[p, h]
                if n == 0:
                    lg = jnp.where(col >= w, lg, NEG)
                m = jnp.max(lg, axis=-1, keepdims=True)
                pe = jnp.exp(lg - m)
                l = jnp.sum(pe, axis=-1, keepdims=True)
                pv = _mm(pe.astype(BF16), vw)
                res.append((m, l, pv))
            mc = jnp.where(h0, res[0][0], res[1][0])
            lc = jnp.where(h0, res[0][1], res[1][1])
            ac = jnp.where(h0, res[0][2], res[1][2])
            tok = src_index(r + dil * w * n)
            mo = mrun[tok, :]
            mn = jnp.maximum(mo, mc)
            ea = jnp.exp(mo - mn)
            eb = jnp.exp(mc - mn)
            lrun[tok, :] = ea * lrun[tok, :] + eb * lc
            arun[tok, :] = ea * arun[tok, :] + eb * ac
            mrun[tok, :] = mn

    def out_body(c, carry):
        r0 = pl.multiple_of(c * rows, rows)
        o_ref[pl.ds(r0, rows), :] = (arun[pl.ds(r0, rows), :] / lrun[pl.ds(r0, rows), :]).astype(o_ref.dtype)
        return carry

    lax.fori_loop(0, s // rows, out_body, 0)


def _attention(qkv, q_gain, k_gain, tab, batch, seq):
    t = qkv.shape[0]
    npair = A_HEADS // 2
    qg = (jnp.tile(q_gain.astype(F32), 2) * HEAD_DIM ** -0.5).reshape(1, LANES)
    kg = jnp.tile(k_gain.astype(F32), 2).reshape(1, LANES)
    spad = seq + BAND
    return pl.pallas_call(
        _attn_kernel,
        grid=(batch, npair),
        in_specs=[
            pl.BlockSpec((seq, LANES), lambda b, h: (b, h)),
            pl.BlockSpec((seq, LANES), lambda b, h: (b, npair + h)),
            pl.BlockSpec((seq, LANES), lambda b, h: (b, 2 * npair + h)),
            pl.BlockSpec((1, LANES), lambda b, h: (0, 0)),
            pl.BlockSpec((1, LANES), lambda b, h: (0, 0)),
            pl.BlockSpec((len(A_PATTERNS), 2, BAND, 2 * BAND), lambda b, h: (0, h, 0, 0)),
        ],
        out_specs=pl.BlockSpec((seq, LANES), lambda b, h: (b, h)),
        out_shape=jax.ShapeDtypeStruct((t, A_WIDTH), BF16),
        scratch_shapes=[
            pltpu.VMEM((seq, LANES), F32),
            pltpu.VMEM((seq, LANES), F32),
            pltpu.VMEM((seq, LANES), F32),
            pltpu.VMEM((spad, LANES), BF16),
            pltpu.VMEM((spad, LANES), BF16),
            pltpu.VMEM((spad, LANES), BF16),
            pltpu.VMEM((spad, LANES), BF16),
            pltpu.VMEM((seq, LANES), F32),
            pltpu.VMEM((seq, LANES), F32),
            pltpu.VMEM((seq, LANES), F32),
        ],
        compiler_params=_params(("parallel", "parallel")),
        name="dilated_attn",
    )(qkv, qkv, qkv, qg, kg, tab)


B_SUB = 16
B_ROWS = 4 * B_CHUNK
B_NSUB = B_ROWS // B_SUB
B_PER = B_CHUNK // B_SUB
B_PAIRS = tuple((i, j) for i in range(B_NSUB) for j in range(i) if i // B_PER == j // B_PER)


def _hgrn_consts():
    c = B_CHUNK
    rows = B_ROWS
    hk = B_QK_WIDTH
    hv = B_WIDTH
    sub = B_SUB
    nsub = B_NSUB
    npair = len(B_PAIRS)
    tri = np.kron(np.eye(rows // c, dtype=np.float32), np.tril(np.ones((c, c), np.float32)))
    gt = np.zeros((B_PER * B_HEADS * sub, sub * hk), np.float32)
    bd = np.zeros((hv, hk), np.float32)
    hm = np.zeros((hv, hv), np.float32)
    hmk = np.zeros((B_HEADS, hk), np.float32)
    hmv = np.zeros((B_HEADS, hv), np.float32)
    md = np.zeros((rows, B_PER * B_HEADS * sub), np.float32)
    mo = np.zeros((rows, B_HEADS * npair * sub), np.float32)
    for h in range(B_HEADS):
        bd[h * B_VAL_DIM:(h + 1) * B_VAL_DIM, h * B_KEY_DIM:(h + 1) * B_KEY_DIM] = 1.0
        hm[h * B_VAL_DIM:(h + 1) * B_VAL_DIM, h * B_VAL_DIM:(h + 1) * B_VAL_DIM] = 1.0
        hmk[h, h * B_KEY_DIM:(h + 1) * B_KEY_DIM] = 1.0
        hmv[h, h * B_VAL_DIM:(h + 1) * B_VAL_DIM] = 1.0
        for i in range(B_PER):
            for s in range(sub):
                row = (i * B_HEADS + h) * sub + s
                gt[row, s * hk + h * B_KEY_DIM:s * hk + (h + 1) * B_KEY_DIM] = 1.0
        for i in range(nsub):
            i4 = i % B_PER
            md[i * sub:(i + 1) * sub, (i4 * B_HEADS + h) * sub:(i4 * B_HEADS + h + 1) * sub] = 1.0
        for p, (i, j) in enumerate(B_PAIRS):
            mo[i * sub:(i + 1) * sub, (h * npair + p) * sub:(h * npair + p + 1) * sub] = 1.0
    return tri, gt, bd, hm, hmk, hmv, md, mo


def _hgrn_kernel(qb_ref, fb_ref, ib_ref, og_ref, lbl_ref, onorm_ref, tri_ref, gt_ref, bd_ref, hm_ref,
                 hmk_ref, hmv_ref, md_ref, mo_ref, o_ref, st, dstack, bsc, qsc, ksc, osc, *, layer):
    s = qb_ref.shape[0]
    c = B_CHUNK
    rows = B_ROWS
    hk = B_QK_WIDTH
    sub = B_SUB
    nsub = B_NSUB

    st[...] = jnp.zeros(st.shape, F32)
    lbl = lbl_ref[...]
    e = jnp.exp(lbl - jnp.max(lbl, axis=0, keepdims=True))
    lb = jnp.sum(e[0:layer + 1, :], axis=0, keepdims=True) / jnp.sum(e, axis=0, keepdims=True)
    tri = tri_ref[...]

    def block(bi, carry):
        r0 = pl.multiple_of(bi * rows, rows)
        qr = qb_ref[pl.ds(r0, rows), :]
        q = qr * jax.nn.sigmoid(qr) * (B_KEY_DIM ** -0.5)
        f = lb + (1.0 - lb) * jax.nn.sigmoid(fb_ref[pl.ds(r0, rows), :])
        g1, g2, g3 = _split3(jnp.log(f))
        bcum = _mm(tri, g1) + _mm(tri, g2) + _mm(tri, g3)
        bsc[...] = bcum
        qsc[...] = q
        ksc[...] = 1.0 - f
        trow = lax.broadcasted_iota(I32, (rows, 1), 0)

        def vrows(lo, n):
            return ib_ref[pl.ds(r0 + lo, n), :]

        def rows_of(ref, off):
            return jnp.concatenate(
                [jnp.broadcast_to(ref[i * sub + off:i * sub + off + 1, :], (sub, ref.shape[1])) for i in range(nsub)],
                axis=0)

        rho = jnp.concatenate(
            [jnp.zeros((sub, hk), F32) if i % B_PER == 0
             else jnp.broadcast_to(bsc[i * sub - 1:i * sub, :], (sub, hk)) for i in range(nsub)], axis=0)
        qt = (qsc[...] * jnp.exp(bsc[...] - rho)).astype(BF16)
        kst = jnp.concatenate(
            [ksc[j * sub:(j + 1) * sub, :] * jnp.exp(bsc[i * sub - 1:i * sub, :] - bsc[j * sub:(j + 1) * sub, :])
             for i, j in B_PAIRS], axis=0)
        vst = jnp.concatenate([vrows(j * sub, sub) for _, j in B_PAIRS], axis=0)
        kall = jnp.concatenate([kst * hmk_ref[h:h + 1, :] for h in range(B_HEADS)], axis=0).astype(BF16)
        vall = jnp.concatenate([vst * hmv_ref[h:h + 1, :] for h in range(B_HEADS)], axis=0).astype(BF16)
        s_off = _nt(qt, kall) * mo_ref[...]
        osc[...] = _mm(s_off.astype(BF16), vall)

        for sp in range(sub):
            ok = (trow & (sub - 1)) >= sp
            ex = jnp.exp(jnp.where(ok, bsc[...] - rows_of(bsc, sp), NEG))
            dv = (qsc[...] * rows_of(ksc, sp)) * ex
            dstack[:, sp * hk:(sp + 1) * hk] = dv.astype(BF16)
        s_dia = (_nt(dstack[...], gt_ref[...]) * md_ref[...]).astype(BF16)

        for cc in range(rows // c):
            lo, hi = cc * c, (cc + 1) * c
            vdia = jnp.concatenate(
                [vrows(lo + i * sub, sub) * hmv_ref[h:h + 1, :] for i in range(B_PER) for h in range(B_HEADS)],
                axis=0).astype(BF16)
            stv = st[...]
            qe = (qsc[lo:hi, :] * jnp.exp(bsc[lo:hi, :])).astype(BF16)
            osc[lo:hi, :] += _mm(s_dia[lo:hi, :], vdia) + _nt(qe, stv.astype(BF16))
            blast = bsc[hi - 1:hi, :]
            kd = ksc[lo:hi, :] * jnp.exp(blast - bsc[lo:hi, :])
            st[...] = stv * jnp.exp(blast) + _tn(vrows(lo, c).astype(BF16), kd.astype(BF16)) * bd_ref[...]

        o = osc[...]
        o2 = o * o
        o2h = o2.astype(BF16)
        o2l = (o2 - o2h.astype(F32)).astype(BF16)
        ms = (_mm(o2h, hm_ref[...]) + _mm(o2l, hm_ref[...])) * (1.0 / B_VAL_DIM)
        ogr = og_ref[pl.ds(r0, rows), :]
        y = o * lax.rsqrt(ms + EPS) * onorm_ref[...] * (ogr * jax.nn.sigmoid(ogr))
        o_ref[pl.ds(r0, rows), :] = y.astype(o_ref.dtype)
        return carry

    lax.fori_loop(0, s // rows, block, 0)


def _hgrn(qf, io, lb_logits, layer, out_norm, batch, seq):
    t = qf.shape[0]
    tri, gt, bd, hm, hmk, hmv, md, mo = _hgrn_consts()
    nslot = lb_logits.shape[0]
    const = lambda a, dt: jnp.asarray(a, dt)
    full = lambda shp: pl.BlockSpec(shp, lambda b: (0,) * len(shp))
    return pl.pallas_call(
        functools.partial(_hgrn_kernel, layer=layer),
        grid=(batch,),
        in_specs=[
            pl.BlockSpec((seq, B_QK_WIDTH), lambda b: (b, 0)),
            pl.BlockSpec((seq, B_QK_WIDTH), lambda b: (b, 1)),
            pl.BlockSpec((seq, B_WIDTH), lambda b: (b, 0)),
            pl.BlockSpec((seq, B_WIDTH), lambda b: (b, 1)),
            full((nslot, B_QK_WIDTH)),
            full((1, B_WIDTH)),
            full(tri.shape), full(gt.shape), full(bd.shape), full(hm.shape),
            full(hmk.shape), full(hmv.shape), full(md.shape), full(mo.shape),
        ],
        out_specs=pl.BlockSpec((seq, B_WIDTH), lambda b: (b, 0)),
        out_shape=jax.ShapeDtypeStruct((t, B_WIDTH), BF16),
        scratch_shapes=[
            pltpu.VMEM((B_WIDTH, B_QK_WIDTH), F32),
            pltpu.VMEM((B_ROWS, B_SUB * B_QK_WIDTH), BF16),
            pltpu.VMEM((B_ROWS, B_QK_WIDTH), F32),
            pltpu.VMEM((B_ROWS, B_QK_WIDTH), F32),
            pltpu.VMEM((B_ROWS, B_QK_WIDTH), F32),
            pltpu.VMEM((B_ROWS, B_WIDTH), F32),
        ],
        compiler_params=_params(("parallel",)),
        name="hgrn2",
    )(qf, qf, io, io, lb_logits.astype(F32), out_norm.astype(F32).reshape(1, B_WIDTH),
      const(tri, BF16), const(gt, BF16), const(bd, F32), const(hm, BF16),
      const(hmk, F32), const(hmv, F32), const(md, F32), const(mo, F32))


def _head_ms(t, masks):
    sq = t * t
    ms = jnp.zeros_like(t)
    for mk in masks:
        ms = jnp.where(mk, jnp.sum(jnp.where(mk, sq, 0.0), axis=-1, keepdims=True), ms)
    return ms * (1.0 / HEAD_DIM)


def _mem_kernel(mem_ref, mg_ref, wkv_ref, q_ref, qg_ref, kg_ref, o_ref, ksc, vsc):
    s = q_ref.shape[0]
    lane = lax.broadcasted_iota(I32, (1, C_WIDTH), 1)
    masks = [(lane >= h * HEAD_DIM) & (lane < (h + 1) * HEAD_DIM) for h in range(C_HEADS)]

    m = mem_ref[0]
    ms = jnp.mean(m * m, axis=-1, keepdims=True)
    mn = (m * lax.rsqrt(ms + EPS) * mg_ref[...]).astype(BF16)
    kv = _mm(mn, wkv_ref[...])
    km = kv[:, :C_WIDTH]
    k = km * lax.rsqrt(_head_ms(km, masks) + EPS) * kg_ref[...]
    for h in range(C_HEADS):
        ksc[h] = jnp.where(masks[h], k, 0.0).astype(BF16)
    vsc[...] = kv[:, C_WIDTH:].astype(BF16)

    rows = 256

    def body(c, carry):
        r0 = pl.multiple_of(c * rows, rows)
        q = q_ref[pl.ds(r0, rows), :].astype(F32)
        qn = (q * lax.rsqrt(_head_ms(q, masks) + EPS) * qg_ref[...]).astype(BF16)
        out = jnp.zeros((rows, C_WIDTH), F32)
        for h in range(C_HEADS):
            lg = _nt(qn, ksc[h])
            mx = jnp.max(lg, axis=-1, keepdims=True)
            pe = jnp.exp(lg - mx)
            l = jnp.sum(pe, axis=-1, keepdims=True)
            out = jnp.where(masks[h], _mm(pe.astype(BF16), vsc[...]) / l, out)
        o_ref[pl.ds(r0, rows), :] = out.astype(o_ref.dtype)
        return carry

    lax.fori_loop(0, s // rows, body, 0)


def _memory(mem, mem_g, wkv_bf16, qc, q_gain, k_gain, seq):
    batch, mlen, d = mem.shape
    t = qc.shape[0]
    qg = (jnp.tile(q_gain.astype(F32), C_HEADS) * HEAD_DIM ** -0.5).reshape(1, C_WIDTH)
    kg = jnp.tile(k_gain.astype(F32), C_HEADS).reshape(1, C_WIDTH)
    return pl.pallas_call(
        _mem_kernel,
        grid=(batch,),
        in_specs=[
            pl.BlockSpec((1, mlen, d), lambda b: (b, 0, 0)),
            pl.BlockSpec((1, d), lambda b: (0, 0)),
            pl.BlockSpec(wkv_bf16.shape, lambda b: (0, 0)),
            pl.BlockSpec((seq, C_WIDTH), lambda b: (b, 0)),
            pl.BlockSpec((1, C_WIDTH), lambda b: (0, 0)),
            pl.BlockSpec((1, C_WIDTH), lambda b: (0, 0)),
        ],
        out_specs=pl.BlockSpec((seq, C_WIDTH), lambda b: (b, 0)),
        out_shape=jax.ShapeDtypeStruct((t, C_WIDTH), BF16),
        scratch_shapes=[
            pltpu.VMEM((C_HEADS, mlen, C_WIDTH), BF16),
            pltpu.VMEM((mlen, C_WIDTH), BF16),
        ],
        compiler_params=_params(("parallel",)),
        name="mem_xattn",
    )(mem, mem_g.astype(F32).reshape(1, d), wkv_bf16, qc, qg, kg)


def _merge_kernel(x_ref, ya_ref, yb_ref, yc_ref, gate_ref, wa_ref, wb_ref, wc_ref, wo_ref, fg_ref,
                  wrh_ref, wrl_ref, wsu_ref, wsd_ref, x2_ref, up_ref, st_ref):
    d = x_ref.shape[1]
    mixed = jnp.zeros(x_ref.shape, F32)
    for i, (y_ref, w_ref) in enumerate(((ya_ref, wa_ref), (yb_ref, wb_ref), (yc_ref, wc_ref))):
        g = jax.nn.sigmoid(gate_ref[:, i * d:(i + 1) * d].astype(F32))
        mixed = mixed + g * _mm(y_ref[...], w_ref[...])
    x1 = x_ref[...] + _mm(mixed.astype(BF16), wo_ref[...])
    ms = jnp.mean(x1 * x1, axis=-1, keepdims=True)
    u = x1 * lax.rsqrt(ms + EPS) * fg_ref[...]
    ub = u.astype(BF16)
    ul = (u - ub.astype(F32)).astype(BF16)
    logits = _nt(wrh_ref[...], ub) + _nt(wrl_ref[...], ub) + _nt(wrh_ref[...], ul)
    st_ref[...] = jax.nn.sigmoid(logits)
    hid = _mm(ub, wsu_ref[...])
    nh = hid.shape[1] // 2
    act = (jax.nn.silu(hid[:, :nh]) * hid[:, nh:]).astype(BF16)
    x2_ref[...] = x1 + _mm(act, wsd_ref[...])
    up_ref[...] = ub


def _merge(x2d, ya, yb, yc, gates, wa, wb, wc, wo, fg, wr_hi, wr_lo, wsu, wsd):
    t, d = x2d.shape
    tm = TM_PROJ
    row = lambda w: pl.BlockSpec((tm, w), lambda i: (i, 0))
    full = lambda a: pl.BlockSpec(a.shape, lambda i: (0,) * a.ndim)
    fg2 = fg.astype(F32).reshape(1, d)
    return pl.pallas_call(
        _merge_kernel,
        grid=(t // tm,),
        in_specs=[row(d), row(A_WIDTH), row(B_WIDTH), row(C_WIDTH), row(3 * d),
                  full(wa), full(wb), full(wc), full(wo), full(fg2), full(wr_hi), full(wr_lo),
                  full(wsu), full(wsd)],
        out_specs=[row(d), row(d), pl.BlockSpec((N_EXPERTS, tm), lambda i: (0, i))],
        out_shape=[jax.ShapeDtypeStruct((t, d), F32),
                   jax.ShapeDtypeStruct((t, d), BF16),
                   jax.ShapeDtypeStruct((N_EXPERTS, t), F32)],
        compiler_params=_params(("parallel",)),
        name="merge_router_shared",
    )(x2d, ya, yb, yc, gates, wa, wb, wc, wo, fg2, wr_hi, wr_lo, wsu, wsd)


def _row_to_expert():
    per_group = N_EXPERTS // N_GROUPS
    rows = np.arange(N_EXPERTS)
    return (rows % N_GROUPS) * per_group + rows // N_GROUPS


def _route_kernel(s_ref, bias_ref, eidx_ref, ut_ref, ones_ref, ls_ref,
                  idx_ref, w_ref, lpos_ref, seg_ref, loff_ref, car_ref, carry):
    tr = s_ref.shape[1]
    per_group = N_EXPERTS // N_GROUPS
    ng = N_GROUPS
    nrep = tr // LANES
    ninf = -jnp.inf

    @pl.when(pl.program_id(0) == 0)
    def _():
        carry[...] = jnp.zeros(carry.shape, F32)

    s_all = s_ref[...]
    bias = jnp.concatenate([bias_ref[...]] * nrep, axis=1)
    eidx = jnp.concatenate([eidx_ref[...]] * nrep, axis=1)
    sv = [s_all[i * ng:(i + 1) * ng, :] for i in range(per_group)]
    ev = [eidx[i * ng:(i + 1) * ng, :] for i in range(per_group)]
    cv = [sv[i] + bias[i * ng:(i + 1) * ng, :] for i in range(per_group)]

    def vmax(xs):
        out = xs[0]
        for x in xs[1:]:
            out = jnp.maximum(out, x)
        return out

    def vmin(xs):
        out = xs[0]
        for x in xs[1:]:
            out = jnp.minimum(out, x)
        return out

    m1 = vmax(cv)
    i1 = vmin([jnp.where(cv[i] == m1, i, per_group) for i in range(per_group)])
    m2 = vmax([jnp.where(i1 == i, ninf, cv[i]) for i in range(per_group)])
    gs = m1 + m2
    giota = lax.broadcasted_iota(I32, (ng, tr), 0)
    gmask = jnp.zeros((ng, tr), jnp.bool_)
    for _ in range(TOPK_GROUPS):
        mx = jnp.max(gs, axis=0, keepdims=True)
        pick = jnp.min(jnp.where(gs == mx, giota, ng), axis=0, keepdims=True)
        sel = giota == pick
        gmask = jnp.logical_or(gmask, sel)
        gs = jnp.where(sel, ninf, gs)

    cm = [jnp.where(gmask, cv[i], ninf) for i in range(per_group)]
    idx_out = jnp.zeros((TOP_K, tr), I32)
    w_out = jnp.zeros((TOP_K, tr), F32)
    jiota = lax.broadcasted_iota(I32, (TOP_K, tr), 0)
    for j in range(TOP_K):
        mx = jnp.max(vmax(cm), axis=0, keepdims=True)
        emin = jnp.min(vmin([jnp.where(cm[i] == mx, ev[i], N_EXPERTS) for i in range(per_group)]),
                       axis=0, keepdims=True)
        hit = [ev[i] == emin for i in range(per_group)]
        ssel = hit_sum = None
        for i in range(per_group):
            term = jnp.where(hit[i], sv[i], 0.0)
            hit_sum = term if hit_sum is None else hit_sum + term
        ssel = jnp.sum(hit_sum, axis=0, keepdims=True)
        cm = [jnp.where(hit[i], ninf, cm[i]) for i in range(per_group)]
        idx_out = jnp.where(jiota == j, emin, idx_out)
        w_out = jnp.where(jiota == j, ssel, w_out)
    idx_ref[...] = idx_out
    w_ref[...] = ROUTED_SCALE * w_out / jnp.sum(w_out, axis=0, keepdims=True)

    chosen = [jnp.where(jnp.logical_and(gmask, cm[i] == ninf), 1.0, 0.0) for i in range(per_group)]
    mh = jnp.concatenate(chosen, axis=0).astype(BF16)
    cnt = _mm(mh, ones_ref[...])
    seg8 = ((cnt.astype(I32) + (SUBLANES - 1)) // SUBLANES) * SUBLANES
    seg8f = seg8.astype(F32)
    loff = _mm(ls_ref[...], seg8f.astype(BF16))
    lfull = _mm(mh, ut_ref[...]) + jnp.concatenate([loff] * nrep, axis=1)
    lv = [lfull[i * ng:(i + 1) * ng, :] for i in range(per_group)]
    lpos = jnp.zeros((TOP_K, tr), F32)
    for j in range(TOP_K):
        ej = idx_out[j:j + 1, :]
        acc = None
        for i in range(per_group):
            term = jnp.where(ev[i] == ej, lv[i], 0.0)
            acc = term if acc is None else acc + term
        lpos = jnp.where(jiota == j, jnp.sum(acc, axis=0, keepdims=True), lpos)
    lpos_ref[...] = lpos.astype(I32)
    cr = carry[...]
    seg_ref[0] = seg8
    loff_ref[0] = loff.astype(I32)
    car_ref[0] = cr.astype(I32)
    carry[...] = cr + seg8f


def _route(s_t, bias_rows, t):
    tr = TB_MOE
    nt = t // tr
    r2e = _row_to_expert()
    eidx = jnp.asarray(np.broadcast_to(r2e[:, None], (N_EXPERTS, LANES)).astype(np.int32))
    bias = jnp.broadcast_to(bias_rows.astype(F32)[:, None], (N_EXPERTS, LANES))
    ut = jnp.asarray(np.triu(np.ones((tr, tr), np.float32), k=1), BF16)
    ls = jnp.asarray(np.tril(np.ones((N_EXPERTS, N_EXPERTS), np.float32), k=-1), BF16)
    ones = jnp.ones((tr, LANES), BF16)
    full = lambda a: pl.BlockSpec(a.shape, lambda i: (0,) * a.ndim)
    tile = pl.BlockSpec((TOP_K, tr), lambda i: (0, i))
    tab = pl.BlockSpec((1, N_EXPERTS, LANES), lambda i: (i, 0, 0))
    tab_shape = jax.ShapeDtypeStruct((nt, N_EXPERTS, LANES), I32)
    return pl.pallas_call(
        _route_kernel,
        grid=(nt,),
        in_specs=[pl.BlockSpec((N_EXPERTS, tr), lambda i: (0, i)), full(bias), full(eidx), full(ut), full(ones),
                  full(ls)],
        out_specs=[tile, tile, tile, tab, tab, tab],
        out_shape=[jax.ShapeDtypeStruct((TOP_K, t), I32),
                   jax.ShapeDtypeStruct((TOP_K, t), F32),
                   jax.ShapeDtypeStruct((TOP_K, t), I32),
                   tab_shape, tab_shape, tab_shape],
        scratch_shapes=[pltpu.VMEM((N_EXPERTS, LANES), F32)],
        compiler_params=_params(("arbitrary",)),
        name="route_topk_layout",
    )(s_t, bias, eidx, ut, ones, ls)


def _piece_copies(grow_ref, used_ref, tile, local_ref, hbm_ref, sem, to_hbm):
    base = tile * PIECES_MOE
    used = used_ref[tile]
    group = 4

    def piece(p, carry):
        loc = local_ref.at[pl.ds(pl.multiple_of(p * SUBLANES, SUBLANES), SUBLANES)]
        glb = hbm_ref.at[pl.ds(pl.multiple_of(grow_ref[base + p], SUBLANES), SUBLANES)]
        if to_hbm:
            pltpu.make_async_copy(loc, glb, sem).start()
        else:
            pltpu.make_async_copy(glb, loc, sem).start()
        return carry

    def pieces(g, carry):
        for u in range(group):
            piece(g * group + u, carry)
        return carry

    whole = used // group
    lax.fori_loop(0, whole, pieces, 0)
    lax.fori_loop(whole * group, used, piece, 0)


def _wait_rows(n8, local_ref, hbm_ref, sem):
    for bit in USED_BITS:
        @pl.when((n8 & bit) != 0)
        def _():
            size = SUBLANES * bit
            pltpu.make_async_copy(local_ref.at[pl.ds(0, size)], hbm_ref.at[pl.ds(0, size)], sem).wait()


def _dispatch_kernel(grow_ref, used_ref, zrow_ref, nused_ref, lpos_ref, u_ref, xs_hbm,
                     xloc, zbuf, sems, zsem):
    i = pl.program_id(0)
    nt = pl.num_programs(0)
    tb = u_ref.shape[0]
    bm = zbuf.shape[0]
    slot = lax.rem(i, 2)

    def zero_copy(e):
        return pltpu.make_async_copy(zbuf, xs_hbm.at[pl.ds(pl.multiple_of(zrow_ref[e], bm), bm)], zsem)

    @pl.when(i == 0)
    def _():
        zbuf[...] = jnp.zeros(zbuf.shape, zbuf.dtype)

        def zstart(e, carry):
            @pl.when(zrow_ref[e] >= 0)
            def _():
                zero_copy(e).start()
            return carry

        def zwait(e, carry):
            @pl.when(zrow_ref[e] >= 0)
            def _():
                zero_copy(e).wait()
            return carry

        lax.fori_loop(0, N_EXPERTS, zstart, 0)
        lax.fori_loop(0, N_EXPERTS, zwait, 0)

        def tail_copy(b):
            return pltpu.make_async_copy(zbuf, xs_hbm.at[pl.ds(pl.multiple_of(b * bm, bm), bm)], zsem)

        def tstart(b, carry):
            tail_copy(b).start()
            return carry

        def twait(b, carry):
            tail_copy(b).wait()
            return carry

        nblk = xs_hbm.shape[0] // bm
        lax.fori_loop(nused_ref[0], nblk, tstart, 0)
        lax.fori_loop(nused_ref[0], nblk, twait, 0)

    @pl.when(i >= 2)
    def _():
        _wait_rows(used_ref[i - 2], xloc.at[slot], xs_hbm, sems.at[slot])

    used = used_ref[i]
    ub = u_ref[...]
    for rc in range(RL_MOE // RC_MOE):
        @pl.when(rc * (RC_MOE // SUBLANES) < used)
        def _():
            rows = lax.broadcasted_iota(I32, (RC_MOE, tb), 0) + rc * RC_MOE
            onehot = jnp.zeros((RC_MOE, tb), F32)
            for j in range(TOP_K):
                onehot = jnp.where(rows == lpos_ref[j:j + 1, :], 1.0, onehot)
            xloc[slot, rc * RC_MOE:(rc + 1) * RC_MOE, :] = _pack_pairs(_mm(onehot.astype(BF16), ub), True)

    _piece_copies(grow_ref, used_ref, i, xloc.at[slot], xs_hbm, sems.at[slot], True)

    @pl.when(i == nt - 1)
    def _():
        _wait_rows(used, xloc.at[slot], xs_hbm, sems.at[slot])

        @pl.when(i >= 1)
        def _():
            _wait_rows(used_ref[i - 1], xloc.at[1 - slot], xs_hbm, sems.at[1 - slot])


def _dispatch(tables, zrow, n_used, lpos_t, u_bf16, n_rows):
    t, d = u_bf16.shape
    tb = TB_MOE
    half = d // 2
    nsp = len(tables) + 2
    return pl.pallas_call(
        _dispatch_kernel,
        grid_spec=pltpu.PrefetchScalarGridSpec(
            num_scalar_prefetch=nsp,
            grid=(t // tb,),
            in_specs=[pl.BlockSpec((TOP_K, tb), lambda i, *_: (0, i)),
                      pl.BlockSpec((tb, d), lambda i, *_: (i, 0))],
            out_specs=pl.BlockSpec(memory_space=pl.ANY),
            scratch_shapes=[
                pltpu.VMEM((2, RL_MOE, half), U32),
                pltpu.VMEM((BM_EXPERT, half), U32),
                pltpu.SemaphoreType.DMA((2,)),
                pltpu.SemaphoreType.DMA,
            ],
        ),
        out_shape=jax.ShapeDtypeStruct((n_rows, half), U32),
        compiler_params=_params(("arbitrary",)),
        name="dispatch_rows",
    )(*tables, zrow, n_used, lpos_t, u_bf16)


def _expert_kernel(be_ref, nu_ref, xs_ref, wu_ref, wd_ref, ys_ref):
    @pl.when(pl.program_id(0) < nu_ref[0])
    def _():
        xa, xb = _unpack_pairs(xs_ref[...])
        half = xa.shape[1]
        hid = _mm(xa.astype(BF16), wu_ref[0, :half, :]) + _mm(xb.astype(BF16), wu_ref[0, half:, :])
        nh = hid.shape[1] // 2
        act = (jax.nn.silu(hid[:, :nh]) * hid[:, nh:]).astype(BF16)
        ys_ref[...] = _pack_pairs(_mm(act, wd_ref[0]))

    @pl.when(pl.program_id(0) >= nu_ref[0])
    def _():
        ys_ref[...] = jnp.zeros(ys_ref.shape, ys_ref.dtype)


def _experts(block_expert, n_used, xs, wu, wd):
    n_rows, half = xs.shape
    bm = BM_EXPERT
    nblk = n_rows // bm
    d = 2 * half
    last = lambda b, nu: jnp.minimum(b, nu[0] - 1)
    return pl.pallas_call(
        _expert_kernel,
        grid_spec=pltpu.PrefetchScalarGridSpec(
            num_scalar_prefetch=2,
            grid=(nblk,),
            in_specs=[
                pl.BlockSpec((bm, half), lambda b, be, nu: (last(b, nu), 0)),
                pl.BlockSpec((1, d, wu.shape[2]), lambda b, be, nu: (be[last(b, nu)], 0, 0)),
                pl.BlockSpec((1, wd.shape[1], d), lambda b, be, nu: (be[last(b, nu)], 0, 0)),
            ],
            out_specs=pl.BlockSpec((bm, half), lambda b, be, nu: (b, 0)),
        ),
        out_shape=jax.ShapeDtypeStruct((n_rows, half), U32),
        compiler_params=_params(("arbitrary",)),
        name="expert_swiglu",
    )(block_expert, n_used, xs, wu, wd)


def _combine_kernel(grow_ref, used_ref, ys_hbm, lpos_ref, w_ref, x2_ref, o_ref,
                    yloc, acc, sems):
    i = pl.program_id(0)
    nt = pl.num_programs(0)
    tb = x2_ref.shape[0]
    half = yloc.shape[2]
    slot = lax.rem(i, 2)

    def fetch(tile, sl):
        _piece_copies(grow_ref, used_ref, tile, yloc.at[sl], ys_hbm, sems.at[sl], False)

    @pl.when(i == 0)
    def _():
        yloc[...] = jnp.zeros(yloc.shape, yloc.dtype)
        fetch(0, 0)

    @pl.when(i + 1 < nt)
    def _():
        fetch(i + 1, 1 - slot)

    used = used_ref[i]
    _wait_rows(used, yloc.at[slot], ys_hbm, sems.at[slot])

    acc[...] = x2_ref[...]
    for kc in range(RL_MOE // RC_MOE):
        @pl.when(kc * (RC_MOE // SUBLANES) < used)
        def _():
            rows = lax.broadcasted_iota(I32, (RC_MOE, tb), 0) + kc * RC_MOE
            pw = jnp.zeros((RC_MOE, tb), F32)
            for j in range(TOP_K):
                pw = jnp.where(rows == lpos_ref[j:j + 1, :], w_ref[j:j + 1, :], pw)
            pw = pw.astype(BF16)
            ya, yb = _unpack_pairs(yloc[slot, kc * RC_MOE:(kc + 1) * RC_MOE, :])
            acc[:, :half] += _tn(pw, ya.astype(BF16))
            acc[:, half:] += _tn(pw, yb.astype(BF16))
    o_ref[...] = acc[...]


def _combine(tables, ys, lpos_t, w_t, x2):
    t, d = x2.shape
    tb = TB_MOE
    half = d // 2
    nsp = len(tables)
    tok = lambda w: pl.BlockSpec((tb, w), lambda i, *_: (i, 0))
    slots = pl.BlockSpec((TOP_K, tb), lambda i, *_: (0, i))
    return pl.pallas_call(
        _combine_kernel,
        grid_spec=pltpu.PrefetchScalarGridSpec(
            num_scalar_prefetch=nsp,
            grid=(t // tb,),
            in_specs=[pl.BlockSpec(memory_space=pl.ANY), slots, slots, tok(d)],
            out_specs=tok(d),
            scratch_shapes=[
                pltpu.VMEM((2, RL_MOE, half), U32),
                pltpu.VMEM((tb, d), F32),
                pltpu.SemaphoreType.DMA((2,)),
            ],
        ),
        out_shape=jax.ShapeDtypeStruct((t, d), F32),
        compiler_params=_params(("arbitrary",)),
        name="combine_rows",
    )(*tables, ys, lpos_t, w_t, x2)


def kernel(x, mem, attn_norm_g, w_in, q_norm_a, k_norm_a, rel_bias, lb_logits, out_norm_b, mem_norm_g, w_mem_kv,
           q_norm_c, k_norm_c, w_branch_a, w_branch_b, w_branch_c, w_out, ffn_norm_g, w_router, router_bias,
           w_exp_up, w_exp_down, w_shared_up, w_shared_down):
    batch, seq, d = x.shape
    t = batch * seq
    depth = w_in.shape[0]
    tab = _bias_tables(rel_bias)
    r2e = _row_to_expert()
    xf = x.reshape(t, d)
    for layer in range(depth):
        qkv, qf, io, qc, gates = _inproj(xf, attn_norm_g[layer], w_in[layer].astype(BF16))
        ya = _attention(qkv, q_norm_a[layer], k_norm_a[layer], tab, batch, seq)
        yb = _hgrn(qf, io, lb_logits, layer, out_norm_b[layer], batch, seq)
        yc = _memory(mem, mem_norm_g[layer], w_mem_kv[layer].astype(BF16), qc, q_norm_c[layer], k_norm_c[layer], seq)

        wr_rows = w_router[layer].astype(F32).T[r2e]
        wr_hi = wr_rows.astype(BF16)
        wr_lo = (wr_rows - wr_hi.astype(F32)).astype(BF16)
        x2, u_bf16, s_t = _merge(
            xf, ya, yb, yc, gates,
            w_branch_a[layer].astype(BF16), w_branch_b[layer].astype(BF16), w_branch_c[layer].astype(BF16),
            w_out[layer].astype(BF16), ffn_norm_g[layer], wr_hi, wr_lo,
            w_shared_up[layer].astype(BF16), w_shared_down[layer].astype(BF16))

        _, w_t, lpos_t, seg_tab, loff_tab, car_tab = _route(s_t, router_bias[layer][r2e], t)

        bm = BM_EXPERT
        nt = t // TB_MOE
        seg = seg_tab[:, :, 0]
        loff = loff_tab[:, :, 0]
        car = car_tab[:, :, 0]
        total = car[-1] + seg[-1]
        padded = (total + bm - 1) // bm * bm
        pend = jnp.cumsum(padded)
        pstart = pend - padded
        n_rows = (t * TOP_K + N_EXPERTS * (SUBLANES - 1) * nt) // bm * bm + N_EXPERTS * bm
        nblk = n_rows // bm
        block_row = jnp.arange(nblk, dtype=I32) * bm
        block_slot = jnp.minimum(jnp.sum((pend[None, :] <= block_row[:, None]).astype(I32), axis=1), N_EXPERTS - 1)
        block_expert = jnp.asarray(r2e, I32)[block_slot]
        n_used = (pend[-1:] // bm).astype(I32)
        zrow = jnp.where(padded > total, pend - bm, -1).astype(I32)
        seg_end = (loff + seg) // SUBLANES
        piece = jnp.arange(PIECES_MOE, dtype=I32)
        owner = jnp.minimum(jnp.sum((seg_end[:, None, :] <= piece[None, :, None]).astype(I32), axis=2),
                            N_EXPERTS - 1)
        shift = pstart[None, :] + car - loff
        is_owner = owner[:, :, None] == jnp.arange(N_EXPERTS, dtype=I32)[None, None, :]
        grow = jnp.sum(jnp.where(is_owner, shift[:, None, :], 0), axis=2) + piece[None, :] * SUBLANES
        tables = (grow.reshape(-1).astype(I32), seg_end[:, -1].astype(I32))

        xs = _dispatch(tables, zrow, n_used, lpos_t, u_bf16, n_rows)
        ys = _experts(block_expert, n_used, xs, w_exp_up[layer].astype(BF16), w_exp_down[layer].astype(BF16))
        xf = _combine(tables, ys, lpos_t, w_t, x2)
    return xf.reshape(batch, seq, d)
```

```python
import functools
import math

import numpy as np
import jax
import jax.numpy as jnp
from jax import lax
from jax.experimental import pallas as pl
from jax.experimental.pallas import tpu as pltpu

F32 = jnp.float32
BF16 = jnp.bfloat16
I32 = jnp.int32
U32 = jnp.uint32

HEAD_DIM = 64
EPS = 1e-6
A_HEADS = 6
A_PATTERNS = ((128, 1), (512, 4), (2048, 16))
BAND = 128
N_BUCKETS = 32
REL_MAX_DIST = 2048
B_HEADS = 4
B_KEY_DIM = 64
B_VAL_DIM = 96
B_CHUNK = 64
C_HEADS = 4
N_EXPERTS = 64
TOP_K = 8
N_GROUPS = 8
TOPK_GROUPS = 4
EXPERT_HIDDEN = 256
ROUTED_SCALE = 2.5

A_WIDTH = A_HEADS * HEAD_DIM
B_QK_WIDTH = B_HEADS * B_KEY_DIM
B_WIDTH = B_HEADS * B_VAL_DIM
C_WIDTH = C_HEADS * HEAD_DIM

NEG = -1e30
HI_MASK = 0xFFFF0000

LANES = 128
SUBLANES = 8
VMEM_LIMIT = 56 * 1024 * 1024

TM_PROJ = 512
TB_MOE = 256
BM_EXPERT = 1024
RC_MOE = 256
RL_MOE = -(-(TB_MOE * TOP_K + N_EXPERTS * (SUBLANES - 1)) // RC_MOE) * RC_MOE
ALWAYS_MOE = TB_MOE * TOP_K // RC_MOE
PIECES_MOE = RL_MOE // SUBLANES
USED_BITS = (256, 128, 64, 32, 16, 8, 4, 2, 1)


def _params(sem, vmem=VMEM_LIMIT):
    return pltpu.CompilerParams(dimension_semantics=sem, vmem_limit_bytes=vmem)


def _nt(a, b):
    return lax.dot_general(a, b, (((1,), (1,)), ((), ())), preferred_element_type=F32)


def _tn(a, b):
    return lax.dot_general(a, b, (((0,), (0,)), ((), ())), preferred_element_type=F32)


def _mm(a, b):
    return jnp.dot(a, b, preferred_element_type=F32)


def _split3(t):
    t1 = t.astype(BF16)
    r1 = t - t1.astype(F32)
    t2 = r1.astype(BF16)
    t3 = (r1 - t2.astype(F32)).astype(BF16)
    return t1, t2, t3


def _pack_pairs(y, already_bf16=False):
    n = y.shape[1] // 2
    bits = pltpu.bitcast(y if already_bf16 else y.astype(BF16).astype(F32), U32)
    return (bits[:, :n] & U32(HI_MASK)) | (bits[:, n:] >> U32(16))


def _unpack_pairs(p):
    hi = pltpu.bitcast(p & U32(HI_MASK), F32)
    lo = pltpu.bitcast(p << U32(16), F32)
    return hi, lo


def _inproj_kernel(x_ref, g_ref, w_ref, qkv_ref, qf_ref, io_ref, qc_ref, gate_ref):
    x = x_ref[...]
    ms = jnp.mean(x * x, axis=-1, keepdims=True)
    h = (x * lax.rsqrt(ms + EPS) * g_ref[...]).astype(BF16)
    col = 0
    for out in (qkv_ref, qf_ref, io_ref, qc_ref, gate_ref):
        width = out.shape[1]
        for a in range(0, width, 512):
            b = min(a + 512, width)
            out[:, a:b] = _mm(h, w_ref[:, col + a:col + b]).astype(out.dtype)
        col += width


def _inproj(x2d, g, w_bf16):
    t, d = x2d.shape
    widths = (3 * A_WIDTH, 2 * B_QK_WIDTH, 2 * B_WIDTH, C_WIDTH, 3 * d)
    dtypes = (BF16, F32, F32, BF16, BF16)
    tm = TM_PROJ
    return pl.pallas_call(
        _inproj_kernel,
        grid=(t // tm,),
        in_specs=[
            pl.BlockSpec((tm, d), lambda i: (i, 0)),
            pl.BlockSpec((1, d), lambda i: (0, 0)),
            pl.BlockSpec(w_bf16.shape, lambda i: (0, 0), pipeline_mode=pl.Buffered(1)),
        ],
        out_specs=[pl.BlockSpec((tm, w), lambda i: (i, 0)) for w in widths],
        out_shape=[jax.ShapeDtypeStruct((t, w), dt) for w, dt in zip(widths, dtypes)],
        compiler_params=_params(("parallel",)),
        name="inproj",
    )(x2d, g.reshape(1, d), w_bf16)


def _bucket_tables():
    w = BAND
    i = np.arange(w)[:, None]
    c = np.arange(2 * w)[None, :]
    steps = w + i - c
    in_band = (steps >= 0) & (steps <= w)
    max_exact = N_BUCKETS // 2
    tabs = []
    for _, dil in A_PATTERNS:
        d = (np.clip(steps, 0, w) * dil).astype(np.int32)
        ratio = np.maximum(d, 1).astype(np.float32) / np.float32(max_exact)
        large = max_exact + (np.log(ratio) / np.float32(math.log(REL_MAX_DIST / max_exact))
                             * np.float32(N_BUCKETS - max_exact)).astype(np.int32)
        bucket = np.where(d < max_exact, d, np.minimum(large, N_BUCKETS - 1))
        tabs.append(np.where(in_band, bucket, -1))
    return np.stack(tabs).astype(np.int32)


def _bias_kernel(rb_ref, bucket_ref, out_ref):
    h = pl.program_id(1)
    bk = bucket_ref[0]
    acc = jnp.full(bk.shape, NEG, F32)
    for b in range(N_BUCKETS):
        acc = jnp.where(bk == b, rb_ref[b, h], acc)
    out_ref[0, 0] = acc


def _bias_tables(rel_bias):
    buckets = jnp.asarray(_bucket_tables())
    p = len(A_PATTERNS)
    return pl.pallas_call(
        _bias_kernel,
        grid=(p, A_HEADS),
        in_specs=[
            pl.BlockSpec(memory_space=pltpu.SMEM),
            pl.BlockSpec((1, BAND, 2 * BAND), lambda i, h: (i, 0, 0)),
        ],
        out_specs=pl.BlockSpec((1, 1, BAND, 2 * BAND), lambda i, h: (i, h, 0, 0)),
        out_shape=jax.ShapeDtypeStruct((p, A_HEADS, BAND, 2 * BAND), F32),
        compiler_params=_params(("arbitrary", "arbitrary")),
        name="attn_bias",
    )(rel_bias.astype(F32), buckets)


def _attn_kernel(q_ref, k_ref, v_ref, qg_ref, kg_ref, tab_ref, o_ref,
                 qn, kn, vf, qp, kp0, kp1, vp, mrun, lrun, arun):
    s = q_ref.shape[0]
    w = BAND
    lane = lax.broadcasted_iota(I32, (1, LANES), 1)
    h0 = lane < HEAD_DIM
    rows = 256

    def norm_body(c, carry):
        r0 = pl.multiple_of(c * rows, rows)
        for src, dst, g in ((q_ref, qn, qg_ref), (k_ref, kn, kg_ref)):
            t = src[pl.ds(r0, rows), :].astype(F32)
            sq = t * t
            s0 = jnp.sum(jnp.where(h0, sq, 0.0), axis=-1, keepdims=True)
            s1 = jnp.sum(jnp.where(h0, 0.0, sq), axis=-1, keepdims=True)
            ms = jnp.where(h0, s0, s1) * (1.0 / HEAD_DIM)
            dst[pl.ds(r0, rows), :] = t * lax.rsqrt(ms + EPS) * g[...]
        vf[pl.ds(r0, rows), :] = v_ref[pl.ds(r0, rows), :].astype(F32)
        return carry

    lax.fori_loop(0, s // rows, norm_body, 0)

    mrun[...] = jnp.full(mrun.shape, NEG, F32)
    lrun[...] = jnp.zeros(lrun.shape, F32)
    arun[...] = jnp.zeros(arun.shape, F32)
    zpad = jnp.zeros((w, LANES), BF16)
    kp0[0:w, :] = zpad
    kp1[0:w, :] = zpad
    vp[0:w, :] = zpad

    col = lax.broadcasted_iota(I32, (1, 2 * w), 1)

    for p, (window, dil) in enumerate(A_PATTERNS):
        seg = s // dil
        nb = seg // w
        nblk = s // w

        def src_index(start):
            if dil == 1:
                return pl.ds(start, w)
            return pl.ds(start, w, stride=dil)

        for r in range(dil):
            for c in range(nb):
                idx = src_index(r + dil * w * c)
                dst = w + r * seg + c * w
                kv = kn[idx, :]
                qp[dst:dst + w, :] = qn[idx, :].astype(BF16)
                kp0[dst:dst + w, :] = jnp.where(h0, kv, 0.0).astype(BF16)
                kp1[dst:dst + w, :] = jnp.where(h0, 0.0, kv).astype(BF16)
                vp[dst:dst + w, :] = vf[idx, :].astype(BF16)

        for i in range(nblk):
            row0 = i * w
            qb = qp[row0 + w:row0 + 2 * w, :]
            vw = vp[row0:row0 + 2 * w, :]
            n = i % nb
            r = i // nb
            res = []
            for h, kp in ((0, kp0), (1, kp1)):
                kw = kp[row0:row0 + 2 * w, :]
                lg = _nt(qb, kw) + tab_ref[p, h]
                if n == 0:
                    lg = jnp.where(col >= w, lg, NEG)
                m = jnp.max(lg, axis=-1, keepdims=True)
                pe = jnp.exp(lg - m)
                l = jnp.sum(pe, axis=-1, keepdims=True)
                pv = _mm(pe.astype(BF16), vw)
                res.append((m, l, pv))
            mc = jnp.where(h0, res[0][0], res[1][0])
            lc = jnp.where(h0, res[0][1], res[1][1])
            ac = jnp.where(h0, res[0][2], res[1][2])
            tok = src_index(r + dil * w * n)
            mo = mrun[tok, :]
            mn = jnp.maximum(mo, mc)
            ea = jnp.exp(mo - mn)
            eb = jnp.exp(mc - mn)
            lrun[tok, :] = ea * lrun[tok, :] + eb * lc
            arun[tok, :] = ea * arun[tok, :] + eb * ac
            mrun[tok, :] = mn

    def out_body(c, carry):
        r0 = pl.multiple_of(c * rows, rows)
        o_ref[pl.ds(r0, rows), :] = (arun[pl.ds(r0, rows), :] / lrun[pl.ds(r0, rows), :]).astype(o_ref.dtype)
        return carry

    lax.fori_loop(0, s // rows, out_body, 0)


def _attention(qkv, q_gain, k_gain, tab, batch, seq):
    t = qkv.shape[0]
    npair = A_HEADS // 2
    qg = (jnp.tile(q_gain.astype(F32), 2) * HEAD_DIM ** -0.5).reshape(1, LANES)
    kg = jnp.tile(k_gain.astype(F32), 2).reshape(1, LANES)
    spad = seq + BAND
    return pl.pallas_call(
        _attn_kernel,
        grid=(batch, npair),
        in_specs=[
            pl.BlockSpec((seq, LANES), lambda b, h: (b, h)),
            pl.BlockSpec((seq, LANES), lambda b, h: (b, npair + h)),
            pl.BlockSpec((seq, LANES), lambda b, h: (b, 2 * npair + h)),
            pl.BlockSpec((1, LANES), lambda b, h: (0, 0)),
            pl.BlockSpec((1, LANES), lambda b, h: (0, 0)),
            pl.BlockSpec((len(A_PATTERNS), 2, BAND, 2 * BAND), lambda b, h: (0, h, 0, 0)),
        ],
        out_specs=pl.BlockSpec((seq, LANES), lambda b, h: (b, h)),
        out_shape=jax.ShapeDtypeStruct((t, A_WIDTH), BF16),
        scratch_shapes=[
            pltpu.VMEM((seq, LANES), F32),
            pltpu.VMEM((seq, LANES), F32),
            pltpu.VMEM((seq, LANES), F32),
            pltpu.VMEM((spad, LANES), BF16),
            pltpu.VMEM((spad, LANES), BF16),
            pltpu.VMEM((spad, LANES), BF16),
            pltpu.VMEM((spad, LANES), BF16),
            pltpu.VMEM((seq, LANES), F32),
            pltpu.VMEM((seq, LANES), F32),
            pltpu.VMEM((seq, LANES), F32),
        ],
        compiler_params=_params(("parallel", "parallel")),
        name="dilated_attn",
    )(qkv, qkv, qkv, qg, kg, tab)


B_SUB = 16
B_ROWS = 4 * B_CHUNK
B_NSUB = B_ROWS // B_SUB
B_PER = B_CHUNK // B_SUB
B_PAIRS = tuple((i, j) for i in range(B_NSUB) for j in range(i) if i // B_PER == j // B_PER)


def _hgrn_consts():
    c = B_CHUNK
    rows = B_ROWS
    hk = B_QK_WIDTH
    hv = B_WIDTH
    sub = B_SUB
    nsub = B_NSUB
    npair = len(B_PAIRS)
    tri = np.kron(np.eye(rows // c, dtype=np.float32), np.tril(np.ones((c, c), np.float32)))
    gt = np.zeros((B_PER * B_HEADS * sub, sub * hk), np.float32)
    bd = np.zeros((hv, hk), np.float32)
    hm = np.zeros((hv, hv), np.float32)
    hmk = np.zeros((B_HEADS, hk), np.float32)
    hmv = np.zeros((B_HEADS, hv), np.float32)
    md = np.zeros((rows, B_PER * B_HEADS * sub), np.float32)
    mo = np.zeros((rows, B_HEADS * npair * sub), np.float32)
    for h in range(B_HEADS):
        bd[h * B_VAL_DIM:(h + 1) * B_VAL_DIM, h * B_KEY_DIM:(h + 1) * B_KEY_DIM] = 1.0
        hm[h * B_VAL_DIM:(h + 1) * B_VAL_DIM, h * B_VAL_DIM:(h + 1) * B_VAL_DIM] = 1.0
        hmk[h, h * B_KEY_DIM:(h + 1) * B_KEY_DIM] = 1.0
        hmv[h, h * B_VAL_DIM:(h + 1) * B_VAL_DIM] = 1.0
        for i in range(B_PER):
            for s in range(sub):
                row = (i * B_HEADS + h) * sub + s
                gt[row, s * hk + h * B_KEY_DIM:s * hk + (h + 1) * B_KEY_DIM] = 1.0
        for i in range(nsub):
            i4 = i % B_PER
            md[i * sub:(i + 1) * sub, (i4 * B_HEADS + h) * sub:(i4 * B_HEADS + h + 1) * sub] = 1.0
        for p, (i, j) in enumerate(B_PAIRS):
            mo[i * sub:(i + 1) * sub, (h * npair + p) * sub:(h * npair + p + 1) * sub] = 1.0
    return tri, gt, bd, hm, hmk, hmv, md, mo


def _hgrn_kernel(qb_ref, fb_ref, ib_ref, og_ref, lbl_ref, onorm_ref, tri_ref, gt_ref, bd_ref, hm_ref,
                 hmk_ref, hmv_ref, md_ref, mo_ref, o_ref, st, dstack, bsc, qsc, ksc, osc, *, layer):
    s = qb_ref.shape[0]
    c = B_CHUNK
    rows = B_ROWS
    hk = B_QK_WIDTH
    sub = B_SUB
    nsub = B_NSUB

    st[...] = jnp.zeros(st.shape, F32)
    lbl = lbl_ref[...]
    e = jnp.exp(lbl - jnp.max(lbl, axis=0, keepdims=True))
    lb = jnp.sum(e[0:layer + 1, :], axis=0, keepdims=True) / jnp.sum(e, axis=0, keepdims=True)
    tri = tri_ref[...]

    def block(bi, carry):
        r0 = pl.multiple_of(bi * rows, rows)
        qr = qb_ref[pl.ds(r0, rows), :]
        q = qr * jax.nn.sigmoid(qr) * (B_KEY_DIM ** -0.5)
        f = lb + (1.0 - lb) * jax.nn.sigmoid(fb_ref[pl.ds(r0, rows), :])
        g1, g2, g3 = _split3(jnp.log(f))
        bcum = _mm(tri, g1) + _mm(tri, g2) + _mm(tri, g3)
        bsc[...] = bcum
        qsc[...] = q
        ksc[...] = 1.0 - f
        trow = lax.broadcasted_iota(I32, (rows, 1), 0)

        def vrows(lo, n):
            return ib_ref[pl.ds(r0 + lo, n), :]

        def rows_of(ref, off):
            return jnp.concatenate(
                [jnp.broadcast_to(ref[i * sub + off:i * sub + off + 1, :], (sub, ref.shape[1])) for i in range(nsub)],
                axis=0)

        rho = jnp.concatenate(
            [jnp.zeros((sub, hk), F32) if i % B_PER == 0
             else jnp.broadcast_to(bsc[i * sub - 1:i * sub, :], (sub, hk)) for i in range(nsub)], axis=0)
        qt = (qsc[...] * jnp.exp(bsc[...] - rho)).astype(BF16)
        kst = jnp.concatenate(
            [ksc[j * sub:(j + 1) * sub, :] * jnp.exp(bsc[i * sub - 1:i * sub, :] - bsc[j * sub:(j + 1) * sub, :])
             for i, j in B_PAIRS], axis=0)
        vst = jnp.concatenate([vrows(j * sub, sub) for _, j in B_PAIRS], axis=0)
        kall = jnp.concatenate([kst * hmk_ref[h:h + 1, :] for h in range(B_HEADS)], axis=0).astype(BF16)
        vall = jnp.concatenate([vst * hmv_ref[h:h + 1, :] for h in range(B_HEADS)], axis=0).astype(BF16)
        s_off = _nt(qt, kall) * mo_ref[...]
        osc[...] = _mm(s_off.astype(BF16), vall)

        for sp in range(sub):
            ok = (trow & (sub - 1)) >= sp
            ex = jnp.exp(jnp.where(ok, bsc[...] - rows_of(bsc, sp), NEG))
            dv = (qsc[...] * rows_of(ksc, sp)) * ex
            dstack[:, sp * hk:(sp + 1) * hk] = dv.astype(BF16)
        s_dia = (_nt(dstack[...], gt_ref[...]) * md_ref[...]).astype(BF16)

        for cc in range(rows // c):
            lo, hi = cc * c, (cc + 1) * c
            vdia = jnp.concatenate(
                [vrows(lo + i * sub, sub) * hmv_ref[h:h + 1, :] for i in range(B_PER) for h in range(B_HEADS)],
                axis=0).astype(BF16)
            stv = st[...]
            qe = (qsc[lo:hi, :] * jnp.exp(bsc[lo:hi, :])).astype(BF16)
            osc[lo:hi, :] += _mm(s_dia[lo:hi, :], vdia) + _nt(qe, stv.astype(BF16))
            blast = bsc[hi - 1:hi, :]
            kd = ksc[lo:hi, :] * jnp.exp(blast - bsc[lo:hi, :])
            st[...] = stv * jnp.exp(blast) + _tn(vrows(lo, c).astype(BF16), kd.astype(BF16)) * bd_ref[...]

        o = osc[...]
        o2 = o * o
        o2h = o2.astype(BF16)
        o2l = (o2 - o2h.astype(F32)).astype(BF16)
        ms = (_mm(o2h, hm_ref[...]) + _mm(o2l, hm_ref[...])) * (1.0 / B_VAL_DIM)
        ogr = og_ref[pl.ds(r0, rows), :]
        y = o * lax.rsqrt(ms + EPS) * onorm_ref[...] * (ogr * jax.nn.sigmoid(ogr))
        o_ref[pl.ds(r0, rows), :] = y.astype(o_ref.dtype)
        return carry

    lax.fori_loop(0, s // rows, block, 0)


def _hgrn(qf, io, lb_logits, layer, out_norm, batch, seq):
    t = qf.shape[0]
    tri, gt, bd, hm, hmk, hmv, md, mo = _hgrn_consts()
    nslot = lb_logits.shape[0]
    const = lambda a, dt: jnp.asarray(a, dt)
    full = lambda shp: pl.BlockSpec(shp, lambda b: (0,) * len(shp))
    return pl.pallas_call(
        functools.partial(_hgrn_kernel, layer=layer),
        grid=(batch,),
        in_specs=[
            pl.BlockSpec((seq, B_QK_WIDTH), lambda b: (b, 0)),
            pl.BlockSpec((seq, B_QK_WIDTH), lambda b: (b, 1)),
            pl.BlockSpec((seq, B_WIDTH), lambda b: (b, 0)),
            pl.BlockSpec((seq, B_WIDTH), lambda b: (b, 1)),
            full((nslot, B_QK_WIDTH)),
            full((1, B_WIDTH)),
            full(tri.shape), full(gt.shape), full(bd.shape), full(hm.shape),
            full(hmk.shape), full(hmv.shape), full(md.shape), full(mo.shape),
        ],
        out_specs=pl.BlockSpec((seq, B_WIDTH), lambda b: (b, 0)),
        out_shape=jax.ShapeDtypeStruct((t, B_WIDTH), BF16),
        scratch_shapes=[
            pltpu.VMEM((B_WIDTH, B_QK_WIDTH), F32),
            pltpu.VMEM((B_ROWS, B_SUB * B_QK_WIDTH), BF16),
            pltpu.VMEM((B_ROWS, B_QK_WIDTH), F32),
            pltpu.VMEM((B_ROWS, B_QK_WIDTH), F32),
            pltpu.VMEM((B_ROWS, B_QK_WIDTH), F32),
            pltpu.VMEM((B_ROWS, B_WIDTH), F32),
        ],
        compiler_params=_params(("parallel",)),
        name="hgrn2",
    )(qf, qf, io, io, lb_logits.astype(F32), out_norm.astype(F32).reshape(1, B_WIDTH),
      const(tri, BF16), const(gt, BF16), const(bd, F32), const(hm, BF16),
      const(hmk, F32), const(hmv, F32), const(md, F32), const(mo, F32))


def _head_ms(t, masks):
    sq = t * t
    ms = jnp.zeros_like(t)
    for mk in masks:
        ms = jnp.where(mk, jnp.sum(jnp.where(mk, sq, 0.0), axis=-1, keepdims=True), ms)
    return ms * (1.0 / HEAD_DIM)


def _mem_kernel(mem_ref, mg_ref, wkv_ref, q_ref, qg_ref, kg_ref, o_ref, ksc, vsc):
    s = q_ref.shape[0]
    lane = lax.broadcasted_iota(I32, (1, C_WIDTH), 1)
    masks = [(lane >= h * HEAD_DIM) & (lane < (h + 1) * HEAD_DIM) for h in range(C_HEADS)]

    m = mem_ref[0]
    ms = jnp.mean(m * m, axis=-1, keepdims=True)
    mn = (m * lax.rsqrt(ms + EPS) * mg_ref[...]).astype(BF16)
    kv = _mm(mn, wkv_ref[...])
    km = kv[:, :C_WIDTH]
    k = km * lax.rsqrt(_head_ms(km, masks) + EPS) * kg_ref[...]
    for h in range(C_HEADS):
        ksc[h] = jnp.where(masks[h], k, 0.0).astype(BF16)
    vsc[...] = kv[:, C_WIDTH:].astype(BF16)

    rows = 256

    def body(c, carry):
        r0 = pl.multiple_of(c * rows, rows)
        q = q_ref[pl.ds(r0, rows), :].astype(F32)
        qn = (q * lax.rsqrt(_head_ms(q, masks) + EPS) * qg_ref[...]).astype(BF16)
        out = jnp.zeros((rows, C_WIDTH), F32)
        for h in range(C_HEADS):
            lg = _nt(qn, ksc[h])
            mx = jnp.max(lg, axis=-1, keepdims=True)
            pe = jnp.exp(lg - mx)
            l = jnp.sum(pe, axis=-1, keepdims=True)
            out = jnp.where(masks[h], _mm(pe.astype(BF16), vsc[...]) / l, out)
        o_ref[pl.ds(r0, rows), :] = out.astype(o_ref.dtype)
        return carry

    lax.fori_loop(0, s // rows, body, 0)


def _memory(mem, mem_g, wkv_bf16, qc, q_gain, k_gain, seq):
    batch, mlen, d = mem.shape
    t = qc.shape[0]
    qg = (jnp.tile(q_gain.astype(F32), C_HEADS) * HEAD_DIM ** -0.5).reshape(1, C_WIDTH)
    kg = jnp.tile(k_gain.astype(F32), C_HEADS).reshape(1, C_WIDTH)
    return pl.pallas_call(
        _mem_kernel,
        grid=(batch,),
        in_specs=[
            pl.BlockSpec((1, mlen, d), lambda b: (b, 0, 0)),
            pl.BlockSpec((1, d), lambda b: (0, 0)),
            pl.BlockSpec(wkv_bf16.shape, lambda b: (0, 0)),
            pl.BlockSpec((seq, C_WIDTH), lambda b: (b, 0)),
            pl.BlockSpec((1, C_WIDTH), lambda b: (0, 0)),
            pl.BlockSpec((1, C_WIDTH), lambda b: (0, 0)),
        ],
        out_specs=pl.BlockSpec((seq, C_WIDTH), lambda b: (b, 0)),
        out_shape=jax.ShapeDtypeStruct((t, C_WIDTH), BF16),
        scratch_shapes=[
            pltpu.VMEM((C_HEADS, mlen, C_WIDTH), BF16),
            pltpu.VMEM((mlen, C_WIDTH), BF16),
        ],
        compiler_params=_params(("parallel",)),
        name="mem_xattn",
    )(mem, mem_g.astype(F32).reshape(1, d), wkv_bf16, qc, qg, kg)


def _merge_kernel(x_ref, ya_ref, yb_ref, yc_ref, gate_ref, wa_ref, wb_ref, wc_ref, wo_ref, fg_ref,
                  wrh_ref, wrl_ref, wsu_ref, wsd_ref, x2_ref, up_ref, st_ref):
    d = x_ref.shape[1]
    mixed = jnp.zeros(x_ref.shape, F32)
    for i, (y_ref, w_ref) in enumerate(((ya_ref, wa_ref), (yb_ref, wb_ref), (yc_ref, wc_ref))):
        g = jax.nn.sigmoid(gate_ref[:, i * d:(i + 1) * d].astype(F32))
        mixed = mixed + g * _mm(y_ref[...], w_ref[...])
    x1 = x_ref[...] + _mm(mixed.astype(BF16), wo_ref[...])
    ms = jnp.mean(x1 * x1, axis=-1, keepdims=True)
    u = x1 * lax.rsqrt(ms + EPS) * fg_ref[...]
    ub = u.astype(BF16)
    ul = (u - ub.astype(F32)).astype(BF16)
    logits = _nt(wrh_ref[...], ub) + _nt(wrl_ref[...], ub) + _nt(wrh_ref[...], ul)
    st_ref[...] = jax.nn.sigmoid(logits)
    hid = _mm(ub, wsu_ref[...])
    nh = hid.shape[1] // 2
    act = (jax.nn.silu(hid[:, :nh]) * hid[:, nh:]).astype(BF16)
    x2_ref[...] = x1 + _mm(act, wsd_ref[...])
    up_ref[...] = ub


def _merge(x2d, ya, yb, yc, gates, wa, wb, wc, wo, fg, wr_hi, wr_lo, wsu, wsd):
    t, d = x2d.shape
    tm = TM_PROJ
    row = lambda w: pl.BlockSpec((tm, w), lambda i: (i, 0))
    full = lambda a: pl.BlockSpec(a.shape, lambda i: (0,) * a.ndim)
    fg2 = fg.astype(F32).reshape(1, d)
    return pl.pallas_call(
        _merge_kernel,
        grid=(t // tm,),
        in_specs=[row(d), row(A_WIDTH), row(B_WIDTH), row(C_WIDTH), row(3 * d),
                  full(wa), full(wb), full(wc), full(wo), full(fg2), full(wr_hi), full(wr_lo),
                  full(wsu), full(wsd)],
        out_specs=[row(d), row(d), pl.BlockSpec((N_EXPERTS, tm), lambda i: (0, i))],
        out_shape=[jax.ShapeDtypeStruct((t, d), F32),
                   jax.ShapeDtypeStruct((t, d), BF16),
                   jax.ShapeDtypeStruct((N_EXPERTS, t), F32)],
        compiler_params=_params(("parallel",)),
        name="merge_router_shared",
    )(x2d, ya, yb, yc, gates, wa, wb, wc, wo, fg2, wr_hi, wr_lo, wsu, wsd)


def _row_to_expert():
    per_group = N_EXPERTS // N_GROUPS
    rows = np.arange(N_EXPERTS)
    return (rows % N_GROUPS) * per_group + rows // N_GROUPS


def _route_kernel(s_ref, bias_ref, eidx_ref, ut_ref, ones_ref, ls_ref,
                  idx_ref, w_ref, rank_ref, wd_ref, seg_ref, loff_ref, car_ref, carry):
    tr = s_ref.shape[1]
    per_group = N_EXPERTS // N_GROUPS
    ng = N_GROUPS
    nrep = tr // LANES
    ninf = -jnp.inf

    @pl.when(pl.program_id(0) == 0)
    def _():
        carry[...] = jnp.zeros(carry.shape, F32)

    s_all = s_ref[...]
    bias = jnp.concatenate([bias_ref[...]] * nrep, axis=1)
    eidx = jnp.concatenate([eidx_ref[...]] * nrep, axis=1)
    sv = [s_all[i * ng:(i + 1) * ng, :] for i in range(per_group)]
    ev = [eidx[i * ng:(i + 1) * ng, :] for i in range(per_group)]
    cv = [sv[i] + bias[i * ng:(i + 1) * ng, :] for i in range(per_group)]

    def vmax(xs):
        out = xs[0]
        for x in xs[1:]:
            out = jnp.maximum(out, x)
        return out

    def vmin(xs):
        out = xs[0]
        for x in xs[1:]:
            out = jnp.minimum(out, x)
        return out

    m1 = vmax(cv)
    i1 = vmin([jnp.where(cv[i] == m1, i, per_group) for i in range(per_group)])
    m2 = vmax([jnp.where(i1 == i, ninf, cv[i]) for i in range(per_group)])
    gs = m1 + m2
    giota = lax.broadcasted_iota(I32, (ng, tr), 0)
    gmask = jnp.zeros((ng, tr), jnp.bool_)
    for _ in range(TOPK_GROUPS):
        mx = jnp.max(gs, axis=0, keepdims=True)
        pick = jnp.min(jnp.where(gs == mx, giota, ng), axis=0, keepdims=True)
        sel = giota == pick
        gmask = jnp.logical_or(gmask, sel)
        gs = jnp.where(sel, ninf, gs)

    cm = [jnp.where(gmask, cv[i], ninf) for i in range(per_group)]
    idx_out = jnp.zeros((TOP_K, tr), I32)
    w_out = jnp.zeros((TOP_K, tr), F32)
    jiota = lax.broadcasted_iota(I32, (TOP_K, tr), 0)
    for j in range(TOP_K):
        mx = jnp.max(vmax(cm), axis=0, keepdims=True)
        emin = jnp.min(vmin([jnp.where(cm[i] == mx, ev[i], N_EXPERTS) for i in range(per_group)]),
                       axis=0, keepdims=True)
        hit = [ev[i] == emin for i in range(per_group)]
        ssel = hit_sum = None
        for i in range(per_group):
            term = jnp.where(hit[i], sv[i], 0.0)
            hit_sum = term if hit_sum is None else hit_sum + term
        ssel = jnp.sum(hit_sum, axis=0, keepdims=True)
        cm = [jnp.where(hit[i], ninf, cm[i]) for i in range(per_group)]
        idx_out = jnp.where(jiota == j, emin, idx_out)
        w_out = jnp.where(jiota == j, ssel, w_out)
    idx_ref[...] = idx_out
    w_norm = ROUTED_SCALE * w_out / jnp.sum(w_out, axis=0, keepdims=True)
    w_ref[...] = w_norm

    chosen = [jnp.where(jnp.logical_and(gmask, cm[i] == ninf), 1.0, 0.0) for i in range(per_group)]
    mh = jnp.concatenate(chosen, axis=0).astype(BF16)
    cnt = _mm(mh, ones_ref[...])
    seg8 = ((cnt.astype(I32) + (SUBLANES - 1)) // SUBLANES) * SUBLANES
    seg8f = seg8.astype(F32)
    loff = _mm(ls_ref[...], seg8f.astype(BF16))
    rank = _mm(mh, ut_ref[...])
    rank_ref[...] = jnp.where(mh > 0, rank, -1.0).astype(BF16)
    wrows = []
    for i in range(per_group):
        acc = jnp.zeros((ng, tr), F32)
        for j in range(TOP_K):
            acc = jnp.where(ev[i] == idx_out[j:j + 1, :], w_norm[j:j + 1, :], acc)
        wrows.append(acc)
    wd_ref[...] = jnp.concatenate(wrows, axis=0).astype(BF16)
    cr = carry[...]
    seg_ref[0] = seg8
    loff_ref[0] = loff.astype(I32)
    car_ref[0] = cr.astype(I32)
    carry[...] = cr + seg8f


def _route(s_t, bias_rows, t):
    tr = TB_MOE
    nt = t // tr
    r2e = _row_to_expert()
    eidx = jnp.asarray(np.broadcast_to(r2e[:, None], (N_EXPERTS, LANES)).astype(np.int32))
    bias = jnp.broadcast_to(bias_rows.astype(F32)[:, None], (N_EXPERTS, LANES))
    ut = jnp.asarray(np.triu(np.ones((tr, tr), np.float32), k=1), BF16)
    ls = jnp.asarray(np.tril(np.ones((N_EXPERTS, N_EXPERTS), np.float32), k=-1), BF16)
    ones = jnp.ones((tr, LANES), BF16)
    full = lambda a: pl.BlockSpec(a.shape, lambda i: (0,) * a.ndim)
    tile = pl.BlockSpec((TOP_K, tr), lambda i: (0, i))
    dense = pl.BlockSpec((N_EXPERTS, tr), lambda i: (0, i))
    tab = pl.BlockSpec((1, N_EXPERTS, LANES), lambda i: (i, 0, 0))
    tab_shape = jax.ShapeDtypeStruct((nt, N_EXPERTS, LANES), I32)
    return pl.pallas_call(
        _route_kernel,
        grid=(nt,),
        in_specs=[pl.BlockSpec((N_EXPERTS, tr), lambda i: (0, i)), full(bias), full(eidx), full(ut), full(ones),
                  full(ls)],
        out_specs=[tile, tile, dense, dense, tab, tab, tab],
        out_shape=[jax.ShapeDtypeStruct((TOP_K, t), I32),
                   jax.ShapeDtypeStruct((TOP_K, t), F32),
                   jax.ShapeDtypeStruct((N_EXPERTS, t), BF16),
                   jax.ShapeDtypeStruct((N_EXPERTS, t), BF16),
                   tab_shape, tab_shape, tab_shape],
        scratch_shapes=[pltpu.VMEM((N_EXPERTS, LANES), F32)],
        compiler_params=_params(("arbitrary",)),
        name="route_topk_layout",
    )(s_t, bias, eidx, ut, ones, ls)


def _piece_copies(grow_ref, used_ref, tile, local_ref, hbm_ref, sem, to_hbm):
    base = tile * PIECES_MOE
    used = used_ref[tile]
    group = 4

    def piece(p, carry):
        loc = local_ref.at[pl.ds(pl.multiple_of(p * SUBLANES, SUBLANES), SUBLANES)]
        glb = hbm_ref.at[pl.ds(pl.multiple_of(grow_ref[base + p], SUBLANES), SUBLANES)]
        if to_hbm:
            pltpu.make_async_copy(loc, glb, sem).start()
        else:
            pltpu.make_async_copy(glb, loc, sem).start()
        return carry

    def pieces(g, carry):
        for u in range(group):
            piece(g * group + u, carry)
        return carry

    whole = used // group
    lax.fori_loop(0, whole, pieces, 0)
    lax.fori_loop(whole * group, used, piece, 0)


def _wait_rows(n8, local_ref, hbm_ref, sem):
    for bit in USED_BITS:
        @pl.when((n8 & bit) != 0)
        def _():
            size = SUBLANES * bit
            pltpu.make_async_copy(local_ref.at[pl.ds(0, size)], hbm_ref.at[pl.ds(0, size)], sem).wait()


def _chunk_match(rc, seg_ref, rank_ref):
    rows = lax.broadcasted_iota(I32, (RC_MOE, N_EXPERTS), 0) + rc * RC_MOE
    into = rows - seg_ref[0, 0:1, :]
    own = pltpu.bitcast(into, U32) < pltpu.bitcast(seg_ref[0, 1:2, :], U32)
    owner = jnp.where(own, 1.0, 0.0).astype(BF16)
    rel = jnp.sum(jnp.where(own, (into + 1).astype(F32), 0.0), axis=1, keepdims=True) - 1.0
    return owner, _mm(owner, rank_ref[...]) == rel


def _dispatch_kernel(grow_ref, used_ref, zrow_ref, nused_ref, seg_ref, rank_ref, u_ref, xs_hbm,
                     xloc, zbuf, sems, zsem):
    i = pl.program_id(0)
    nt = pl.num_programs(0)
    tb = u_ref.shape[0]
    bm = zbuf.shape[0]
    slot = lax.rem(i, 2)

    def zero_copy(e):
        return pltpu.make_async_copy(zbuf, xs_hbm.at[pl.ds(pl.multiple_of(zrow_ref[e], bm), bm)], zsem)

    @pl.when(i == 0)
    def _():
        zbuf[...] = jnp.zeros(zbuf.shape, zbuf.dtype)

        def zstart(e, carry):
            @pl.when(zrow_ref[e] >= 0)
            def _():
                zero_copy(e).start()
            return carry

        def zwait(e, carry):
            @pl.when(zrow_ref[e] >= 0)
            def _():
                zero_copy(e).wait()
            return carry

        lax.fori_loop(0, N_EXPERTS, zstart, 0)
        lax.fori_loop(0, N_EXPERTS, zwait, 0)

        def tail_copy(b):
            return pltpu.make_async_copy(zbuf, xs_hbm.at[pl.ds(pl.multiple_of(b * bm, bm), bm)], zsem)

        def tstart(b, carry):
            tail_copy(b).start()
            return carry

        def twait(b, carry):
            tail_copy(b).wait()
            return carry

        nblk = xs_hbm.shape[0] // bm
        lax.fori_loop(nused_ref[0], nblk, tstart, 0)
        lax.fori_loop(nused_ref[0], nblk, twait, 0)

    @pl.when(i >= 2)
    def _():
        _wait_rows(used_ref[i - 2], xloc.at[slot], xs_hbm, sems.at[slot])

    used = used_ref[i]
    ub = u_ref[...]
    def sort_chunk(rc):
        _, match = _chunk_match(rc, seg_ref, rank_ref)
        onehot = jnp.where(match, 1.0, 0.0).astype(BF16)
        xloc[slot, rc * RC_MOE:(rc + 1) * RC_MOE, :] = _pack_pairs(_mm(onehot, ub), True)

    for rc in range(ALWAYS_MOE):
        sort_chunk(rc)
    for rc in range(ALWAYS_MOE, RL_MOE // RC_MOE):
        pl.when(rc * (RC_MOE // SUBLANES) < used)(functools.partial(sort_chunk, rc))

    _piece_copies(grow_ref, used_ref, i, xloc.at[slot], xs_hbm, sems.at[slot], True)

    @pl.when(i == nt - 1)
    def _():
        _wait_rows(used, xloc.at[slot], xs_hbm, sems.at[slot])

        @pl.when(i >= 1)
        def _():
            _wait_rows(used_ref[i - 1], xloc.at[1 - slot], xs_hbm, sems.at[1 - slot])


def _dispatch(tables, zrow, n_used, seg_rows, rank_d, u_bf16, n_rows):
    t, d = u_bf16.shape
    tb = TB_MOE
    half = d // 2
    nsp = len(tables) + 2
    return pl.pallas_call(
        _dispatch_kernel,
        grid_spec=pltpu.PrefetchScalarGridSpec(
            num_scalar_prefetch=nsp,
            grid=(t // tb,),
            in_specs=[pl.BlockSpec((1,) + seg_rows.shape[1:], lambda i, *_: (i, 0, 0)),
                      pl.BlockSpec((N_EXPERTS, tb), lambda i, *_: (0, i)),
                      pl.BlockSpec((tb, d), lambda i, *_: (i, 0))],
            out_specs=pl.BlockSpec(memory_space=pl.ANY),
            scratch_shapes=[
                pltpu.VMEM((2, RL_MOE, half), U32),
                pltpu.VMEM((BM_EXPERT, half), U32),
                pltpu.SemaphoreType.DMA((2,)),
                pltpu.SemaphoreType.DMA,
            ],
        ),
        out_shape=jax.ShapeDtypeStruct((n_rows, half), U32),
        compiler_params=_params(("arbitrary",)),
        name="dispatch_rows",
    )(*tables, zrow, n_used, seg_rows, rank_d, u_bf16)


def _expert_kernel(be_ref, nu_ref, xs_ref, wu_ref, wd_ref, ys_ref):
    @pl.when(pl.program_id(0) < nu_ref[0])
    def _():
        xa, xb = _unpack_pairs(xs_ref[...])
        half = xa.shape[1]
        hid = _mm(xa.astype(BF16), wu_ref[0, :half, :]) + _mm(xb.astype(BF16), wu_ref[0, half:, :])
        nh = hid.shape[1] // 2
        act = (jax.nn.silu(hid[:, :nh]) * hid[:, nh:]).astype(BF16)
        ys_ref[...] = _pack_pairs(_mm(act, wd_ref[0]))

    @pl.when(pl.program_id(0) >= nu_ref[0])
    def _():
        ys_ref[...] = jnp.zeros(ys_ref.shape, ys_ref.dtype)


def _experts(block_expert, n_used, xs, wu, wd):
    n_rows, half = xs.shape
    bm = BM_EXPERT
    nblk = n_rows // bm
    d = 2 * half
    last = lambda b, nu: jnp.minimum(b, nu[0] - 1)
    return pl.pallas_call(
        _expert_kernel,
        grid_spec=pltpu.PrefetchScalarGridSpec(
            num_scalar_prefetch=2,
            grid=(nblk,),
            in_specs=[
                pl.BlockSpec((bm, half), lambda b, be, nu: (last(b, nu), 0)),
                pl.BlockSpec((1, d, wu.shape[2]), lambda b, be, nu: (be[last(b, nu)], 0, 0)),
                pl.BlockSpec((1, wd.shape[1], d), lambda b, be, nu: (be[last(b, nu)], 0, 0)),
            ],
            out_specs=pl.BlockSpec((bm, half), lambda b, be, nu: (b, 0)),
        ),
        out_shape=jax.ShapeDtypeStruct((n_rows, half), U32),
        compiler_params=_params(("arbitrary",)),
        name="expert_swiglu",
    )(block_expert, n_used, xs, wu, wd)


def _combine_kernel(grow_ref, used_ref, ys_hbm, seg_ref, rank_ref, wd_ref, x2_ref, o_ref,
                    yloc, acc, sems):
    i = pl.program_id(0)
    nt = pl.num_programs(0)
    tb = x2_ref.shape[0]
    half = yloc.shape[2]
    slot = lax.rem(i, 2)

    def fetch(tile, sl):
        _piece_copies(grow_ref, used_ref, tile, yloc.at[sl], ys_hbm, sems.at[sl], False)

    @pl.when(i == 0)
    def _():
        yloc[...] = jnp.zeros(yloc.shape, yloc.dtype)
        fetch(0, 0)

    @pl.when(i + 1 < nt)
    def _():
        fetch(i + 1, 1 - slot)

    used = used_ref[i]
    _wait_rows(used, yloc.at[slot], ys_hbm, sems.at[slot])

    def weights_of(kc):
        owner, match = _chunk_match(kc, seg_ref, rank_ref)
        return jnp.where(match, _mm(owner, wd_ref[...]), 0.0).astype(BF16)

    base = ALWAYS_MOE * RC_MOE
    pw = jnp.concatenate([weights_of(kc) for kc in range(ALWAYS_MOE)], axis=0)
    ya, yb = _unpack_pairs(yloc[slot, 0:base, :])
    acc[:, :half] = x2_ref[:, :half] + _tn(pw, ya.astype(BF16))
    acc[:, half:] = x2_ref[:, half:] + _tn(pw, yb.astype(BF16))
    for kc in range(ALWAYS_MOE, RL_MOE // RC_MOE):
        @pl.when(kc * (RC_MOE // SUBLANES) < used)
        def _():
            pwk = weights_of(kc)
            yak, ybk = _unpack_pairs(yloc[slot, kc * RC_MOE:(kc + 1) * RC_MOE, :])
            acc[:, :half] += _tn(pwk, yak.astype(BF16))
            acc[:, half:] += _tn(pwk, ybk.astype(BF16))
    o_ref[...] = acc[...]


def _combine(tables, ys, seg_rows, rank_d, w_d, x2):
    t, d = x2.shape
    tb = TB_MOE
    half = d // 2
    nsp = len(tables)
    tok = lambda w: pl.BlockSpec((tb, w), lambda i, *_: (i, 0))
    dense = pl.BlockSpec((N_EXPERTS, tb), lambda i, *_: (0, i))
    segs = pl.BlockSpec((1,) + seg_rows.shape[1:], lambda i, *_: (i, 0, 0))
    return pl.pallas_call(
        _combine_kernel,
        grid_spec=pltpu.PrefetchScalarGridSpec(
            num_scalar_prefetch=nsp,
            grid=(t // tb,),
            in_specs=[pl.BlockSpec(memory_space=pl.ANY), segs, dense, dense, tok(d)],
            out_specs=tok(d),
            scratch_shapes=[
                pltpu.VMEM((2, RL_MOE, half), U32),
                pltpu.VMEM((tb, d), F32),
                pltpu.SemaphoreType.DMA((2,)),
            ],
        ),
        out_shape=jax.ShapeDtypeStruct((t, d), F32),
        compiler_params=_params(("arbitrary",)),
        name="combine_rows",
    )(*tables, ys, seg_rows, rank_d, w_d, x2)


def kernel(x, mem, attn_norm_g, w_in, q_norm_a, k_norm_a, rel_bias, lb_logits, out_norm_b, mem_norm_g, w_mem_kv,
           q_norm_c, k_norm_c, w_branch_a, w_branch_b, w_branch_c, w_out, ffn_norm_g, w_router, router_bias,
           w_exp_up, w_exp_down, w_shared_up, w_shared_down):
    batch, seq, d = x.shape
    t = batch * seq
    depth = w_in.shape[0]
    tab = _bias_tables(rel_bias)
    r2e = _row_to_expert()
    xf = x.reshape(t, d)
    for layer in range(depth):
        qkv, qf, io, qc, gates = _inproj(xf, attn_norm_g[layer], w_in[layer].astype(BF16))
        ya = _attention(qkv, q_norm_a[layer], k_norm_a[layer], tab, batch, seq)
        yb = _hgrn(qf, io, lb_logits, layer, out_norm_b[layer], batch, seq)
        yc = _memory(mem, mem_norm_g[layer], w_mem_kv[layer].astype(BF16), qc, q_norm_c[layer], k_norm_c[layer], seq)

        wr_rows = w_router[layer].astype(F32).T[r2e]
        wr_hi = wr_rows.astype(BF16)
        wr_lo = (wr_rows - wr_hi.astype(F32)).astype(BF16)
        x2, u_bf16, s_t = _merge(
            xf, ya, yb, yc, gates,
            w_branch_a[layer].astype(BF16), w_branch_b[layer].astype(BF16), w_branch_c[layer].astype(BF16),
            w_out[layer].astype(BF16), ffn_norm_g[layer], wr_hi, wr_lo,
            w_shared_up[layer].astype(BF16), w_shared_down[layer].astype(BF16))

        _, _, rank_d, w_d, seg_tab, loff_tab, car_tab = _route(s_t, router_bias[layer][r2e], t)

        bm = BM_EXPERT
        nt = t // TB_MOE
        seg = seg_tab[:, :, 0]
        loff = loff_tab[:, :, 0]
        car = car_tab[:, :, 0]
        total = car[-1] + seg[-1]
        padded = (total + bm - 1) // bm * bm
        pend = jnp.cumsum(padded)
        pstart = pend - padded
        n_rows = (t * TOP_K + N_EXPERTS * (SUBLANES - 1) * nt) // bm * bm + N_EXPERTS * bm
        nblk = n_rows // bm
        block_row = jnp.arange(nblk, dtype=I32) * bm
        block_slot = jnp.minimum(jnp.sum((pend[None, :] <= block_row[:, None]).astype(I32), axis=1), N_EXPERTS - 1)
        block_expert = jnp.asarray(r2e, I32)[block_slot]
        n_used = (pend[-1:] // bm).astype(I32)
        zrow = jnp.where(padded > total, pend - bm, -1).astype(I32)
        seg_end = (loff + seg) // SUBLANES
        piece = jnp.arange(PIECES_MOE, dtype=I32)
        owner = jnp.minimum(jnp.sum((seg_end[:, None, :] <= piece[None, :, None]).astype(I32), axis=2),
                            N_EXPERTS - 1)
        shift = pstart[None, :] + car - loff
        is_owner = owner[:, :, None] == jnp.arange(N_EXPERTS, dtype=I32)[None, None, :]
        grow = jnp.sum(jnp.where(is_owner, shift[:, None, :], 0), axis=2) + piece[None, :] * SUBLANES
        tables = (grow.reshape(-1).astype(I32), seg_end[:, -1].astype(I32))

        seg_rows = jnp.zeros((nt, SUBLANES, N_EXPERTS), I32).at[:, 0, :].set(loff).at[:, 1, :].set(seg)

        xs = _dispatch(tables, zrow, n_used, seg_rows, rank_d, u_bf16, n_rows)
        ys = _experts(block_expert, n_used, xs, w_exp_up[layer].astype(BF16), w_exp_down[layer].astype(BF16))
        xf = _combine(tables, ys, seg_rows, rank_d, w_d, x2)
    return xf.reshape(batch, seq, d)
```

```python
import functools
import math

import numpy as np
import jax
import jax.numpy as jnp
from jax import lax
from jax.experimental import pallas as pl
from jax.experimental.pallas import tpu as pltpu

F32 = jnp.float32
BF16 = jnp.bfloat16
I32 = jnp.int32
U32 = jnp.uint32

HEAD_DIM = 64
EPS = 1e-6
A_HEADS = 6
A_PATTERNS = ((128, 1), (512, 4), (2048, 16))
BAND = 128
N_BUCKETS = 32
REL_MAX_DIST = 2048
B_HEADS = 4
B_KEY_DIM = 64
B_VAL_DIM = 96
B_CHUNK = 64
C_HEADS = 4
N_EXPERTS = 64
TOP_K = 8
N_GROUPS = 8
TOPK_GROUPS = 4
EXPERT_HIDDEN = 256
ROUTED_SCALE = 2.5

A_WIDTH = A_HEADS * HEAD_DIM
B_QK_WIDTH = B_HEADS * B_KEY_DIM
B_WIDTH = B_HEADS * B_VAL_DIM
C_WIDTH = C_HEADS * HEAD_DIM

NEG = -1e30
HI_MASK = 0xFFFF0000

LANES = 128
SUBLANES = 8
VMEM_LIMIT = 56 * 1024 * 1024

TM_PROJ = 512
TB_MOE = 256
BM_EXPERT = 1024
RC_MOE = 256
RL_MOE = -(-(TB_MOE * TOP_K + N_EXPERTS * (SUBLANES - 1)) // RC_MOE) * RC_MOE
ALWAYS_MOE = TB_MOE * TOP_K // RC_MOE + 1
PIECES_MOE = RL_MOE // SUBLANES
USED_BITS = (256, 128, 64, 32, 16, 8, 4, 2, 1)


def _params(sem, vmem=VMEM_LIMIT):
    return pltpu.CompilerParams(dimension_semantics=sem, vmem_limit_bytes=vmem)


def _nt(a, b):
    return lax.dot_general(a, b, (((1,), (1,)), ((), ())), preferred_element_type=F32)


def _tn(a, b):
    return lax.dot_general(a, b, (((0,), (0,)), ((), ())), preferred_element_type=F32)


def _mm(a, b):
    return jnp.dot(a, b, preferred_element_type=F32)


def _split3(t):
    t1 = t.astype(BF16)
    r1 = t - t1.astype(F32)
    t2 = r1.astype(BF16)
    t3 = (r1 - t2.astype(F32)).astype(BF16)
    return t1, t2, t3


def _pack_pairs(y, already_bf16=False):
    n = y.shape[1] // 2
    bits = pltpu.bitcast(y if already_bf16 else y.astype(BF16).astype(F32), U32)
    return (bits[:, :n] & U32(HI_MASK)) | (bits[:, n:] >> U32(16))


def _unpack_pairs(p):
    hi = pltpu.bitcast(p & U32(HI_MASK), F32)
    lo = pltpu.bitcast(p << U32(16), F32)
    return hi, lo


def _inproj_kernel(x_ref, g_ref, w_ref, qkv_ref, qf_ref, io_ref, qc_ref, gate_ref):
    x = x_ref[...]
    ms = jnp.mean(x * x, axis=-1, keepdims=True)
    h = (x * lax.rsqrt(ms + EPS) * g_ref[...]).astype(BF16)
    col = 0
    for out in (qkv_ref, qf_ref, io_ref, qc_ref, gate_ref):
        width = out.shape[1]
        for a in range(0, width, 512):
            b = min(a + 512, width)
            out[:, a:b] = _mm(h, w_ref[:, col + a:col + b]).astype(out.dtype)
        col += width


def _inproj(x2d, g, w_bf16):
    t, d = x2d.shape
    widths = (3 * A_WIDTH, 2 * B_QK_WIDTH, 2 * B_WIDTH, C_WIDTH, 3 * d)
    dtypes = (BF16, F32, F32, BF16, BF16)
    tm = TM_PROJ
    return pl.pallas_call(
        _inproj_kernel,
        grid=(t // tm,),
        in_specs=[
            pl.BlockSpec((tm, d), lambda i: (i, 0)),
            pl.BlockSpec((1, d), lambda i: (0, 0)),
            pl.BlockSpec(w_bf16.shape, lambda i: (0, 0), pipeline_mode=pl.Buffered(1)),
        ],
        out_specs=[pl.BlockSpec((tm, w), lambda i: (i, 0)) for w in widths],
        out_shape=[jax.ShapeDtypeStruct((t, w), dt) for w, dt in zip(widths, dtypes)],
        compiler_params=_params(("parallel",)),
        name="inproj",
    )(x2d, g.reshape(1, d), w_bf16)


def _bucket_tables():
    w = BAND
    i = np.arange(w)[:, None]
    c = np.arange(2 * w)[None, :]
    steps = w + i - c
    in_band = (steps >= 0) & (steps <= w)
    max_exact = N_BUCKETS // 2
    tabs = []
    for _, dil in A_PATTERNS:
        d = (np.clip(steps, 0, w) * dil).astype(np.int32)
        ratio = np.maximum(d, 1).astype(np.float32) / np.float32(max_exact)
        large = max_exact + (np.log(ratio) / np.float32(math.log(REL_MAX_DIST / max_exact))
                             * np.float32(N_BUCKETS - max_exact)).astype(np.int32)
        bucket = np.where(d < max_exact, d, np.minimum(large, N_BUCKETS - 1))
        tabs.append(np.where(in_band, bucket, -1))
    return np.stack(tabs).astype(np.int32)


def _bias_kernel(rb_ref, bucket_ref, out_ref):
    h = pl.program_id(1)
    bk = bucket_ref[0]
    acc = jnp.full(bk.shape, NEG, F32)
    for b in range(N_BUCKETS):
        acc = jnp.where(bk == b, rb_ref[b, h], acc)
    out_ref[0, 0] = acc


def _bias_tables(rel_bias):
    buckets = jnp.asarray(_bucket_tables())
    p = len(A_PATTERNS)
    return pl.pallas_call(
        _bias_kernel,
        grid=(p, A_HEADS),
        in_specs=[
            pl.BlockSpec(memory_space=pltpu.SMEM),
            pl.BlockSpec((1, BAND, 2 * BAND), lambda i, h: (i, 0, 0)),
        ],
        out_specs=pl.BlockSpec((1, 1, BAND, 2 * BAND), lambda i, h: (i, h, 0, 0)),
        out_shape=jax.ShapeDtypeStruct((p, A_HEADS, BAND, 2 * BAND), F32),
        compiler_params=_params(("arbitrary", "arbitrary")),
        name="attn_bias",
    )(rel_bias.astype(F32), buckets)


def _attn_kernel(q_ref, k_ref, v_ref, qg_ref, kg_ref, hmean_ref, tab_ref, o_ref,
                 qn, kn, vf, qp, kp0, kp1, vp, mrun, lrun, arun):
    s = q_ref.shape[0]
    w = BAND
    lane = lax.broadcasted_iota(I32, (1, LANES), 1)
    h0 = lane < HEAD_DIM
    rows = 256

    def norm_body(c, carry):
        r0 = pl.multiple_of(c * rows, rows)
        for src, dst, g in ((q_ref, qn, qg_ref), (k_ref, kn, kg_ref)):
            t = src[pl.ds(r0, rows), :].astype(F32)
            sq = t * t
            sqh = sq.astype(BF16)
            sql = (sq - sqh.astype(F32)).astype(BF16)
            ms = _mm(sqh, hmean_ref[...]) + _mm(sql, hmean_ref[...])
            dst[pl.ds(r0, rows), :] = t * lax.rsqrt(ms + EPS) * g[...]
        vf[pl.ds(r0, rows), :] = v_ref[pl.ds(r0, rows), :].astype(F32)
        return carry

    lax.fori_loop(0, s // rows, norm_body, 0)

    mrun[...] = jnp.full(mrun.shape, NEG, F32)
    lrun[...] = jnp.zeros(lrun.shape, F32)
    arun[...] = jnp.zeros(arun.shape, F32)
    zpad = jnp.zeros((w, LANES), BF16)
    kp0[0:w, :] = zpad
    kp1[0:w, :] = zpad
    vp[0:w, :] = zpad

    col = lax.broadcasted_iota(I32, (1, 2 * w), 1)

    def one_block(p, dil, src_index, row0, n, r):
        qb = qp[row0 + w:row0 + 2 * w, :]
        vw = vp[row0:row0 + 2 * w, :]
        res = []
        for h, kp in ((0, kp0), (1, kp1)):
            kw = kp[row0:row0 + 2 * w, :]
            lg = _nt(qb, kw) + tab_ref[p, h]
            if n == 0:
                lg = jnp.where(col >= w, lg, NEG)
            m = jnp.max(lg, axis=-1, keepdims=True)
            pe = jnp.exp(lg - m)
            l = jnp.sum(pe, axis=-1, keepdims=True)
            pv = _mm(pe.astype(BF16), vw)
            res.append((m, l, pv))
        mc = jnp.where(h0, res[0][0], res[1][0])
        lc = jnp.where(h0, res[0][1], res[1][1])
        ac = jnp.where(h0, res[0][2], res[1][2])
        tok = src_index(r + dil * w * n)
        mo = mrun[tok, :]
        mn = jnp.maximum(mo, mc)
        ea = jnp.exp(mo - mn)
        eb = jnp.exp(mc - mn)
        lrun[tok, :] = ea * lrun[tok, :] + eb * lc
        arun[tok, :] = ea * arun[tok, :] + eb * ac
        mrun[tok, :] = mn

    for p, (window, dil) in enumerate(A_PATTERNS):
        seg = s // dil
        nb = seg // w
        nblk = s // w

        def src_index(start):
            if dil == 1:
                return pl.ds(start, w)
            return pl.ds(start, w, stride=dil)

        for r in range(dil):
            for c in range(nb):
                idx = src_index(r + dil * w * c)
                dst = w + r * seg + c * w
                kv = kn[idx, :]
                qp[dst:dst + w, :] = qn[idx, :].astype(BF16)
                kp0[dst:dst + w, :] = jnp.where(h0, kv, 0.0).astype(BF16)
                kp1[dst:dst + w, :] = jnp.where(h0, 0.0, kv).astype(BF16)
                vp[dst:dst + w, :] = vf[idx, :].astype(BF16)

        for i in range(nblk):
            one_block(p, dil, src_index, i * w, i % nb, i // nb)

    def out_body(c, carry):
        r0 = pl.multiple_of(c * rows, rows)
        o_ref[pl.ds(r0, rows), :] = (arun[pl.ds(r0, rows), :] / lrun[pl.ds(r0, rows), :]).astype(o_ref.dtype)
        return carry

    lax.fori_loop(0, s // rows, out_body, 0)


def _attention(qkv, q_gain, k_gain, tab, batch, seq):
    t = qkv.shape[0]
    npair = A_HEADS // 2
    qg = (jnp.tile(q_gain.astype(F32), 2) * HEAD_DIM ** -0.5).reshape(1, LANES)
    kg = jnp.tile(k_gain.astype(F32), 2).reshape(1, LANES)
    hmean = jnp.asarray(np.kron(np.eye(LANES // HEAD_DIM), np.full((HEAD_DIM, HEAD_DIM), 1.0 / HEAD_DIM)), BF16)
    spad = seq + BAND
    return pl.pallas_call(
        _attn_kernel,
        grid=(batch, npair),
        in_specs=[
            pl.BlockSpec((seq, LANES), lambda b, h: (b, h)),
            pl.BlockSpec((seq, LANES), lambda b, h: (b, npair + h)),
            pl.BlockSpec((seq, LANES), lambda b, h: (b, 2 * npair + h)),
            pl.BlockSpec((1, LANES), lambda b, h: (0, 0)),
            pl.BlockSpec((1, LANES), lambda b, h: (0, 0)),
            pl.BlockSpec((LANES, LANES), lambda b, h: (0, 0)),
            pl.BlockSpec((len(A_PATTERNS), 2, BAND, 2 * BAND), lambda b, h: (0, h, 0, 0)),
        ],
        out_specs=pl.BlockSpec((seq, LANES), lambda b, h: (b, h)),
        out_shape=jax.ShapeDtypeStruct((t, A_WIDTH), BF16),
        scratch_shapes=[
            pltpu.VMEM((seq, LANES), F32),
            pltpu.VMEM((seq, LANES), F32),
            pltpu.VMEM((seq, LANES), F32),
            pltpu.VMEM((spad, LANES), BF16),
            pltpu.VMEM((spad, LANES), BF16),
            pltpu.VMEM((spad, LANES), BF16),
            pltpu.VMEM((spad, LANES), BF16),
            pltpu.VMEM((seq, LANES), F32),
            pltpu.VMEM((seq, LANES), F32),
            pltpu.VMEM((seq, LANES), F32),
        ],
        compiler_params=_params(("parallel", "parallel")),
        name="dilated_attn",
    )(qkv, qkv, qkv, qg, kg, hmean, tab)


B_SUB = 16
B_ROWS = 4 * B_CHUNK
B_NSUB = B_ROWS // B_SUB
B_PER = B_CHUNK // B_SUB
B_PAIRS = tuple((i, j) for i in range(B_NSUB) for j in range(i) if i // B_PER == j // B_PER)


def _hgrn_consts():
    c = B_CHUNK
    rows = B_ROWS
    hk = B_QK_WIDTH
    hv = B_WIDTH
    sub = B_SUB
    nsub = B_NSUB
    npair = len(B_PAIRS)
    tri = np.kron(np.eye(rows // c, dtype=np.float32), np.tril(np.ones((c, c), np.float32)))
    gt = np.zeros((B_PER * B_HEADS * sub, sub * hk), np.float32)
    bd = np.zeros((hv, hk), np.float32)
    hm = np.zeros((hv, hv), np.float32)
    hmk = np.zeros((B_HEADS, hk), np.float32)
    hmv = np.zeros((B_HEADS, hv), np.float32)
    md = np.zeros((rows, B_PER * B_HEADS * sub), np.float32)
    mo = np.zeros((rows, B_HEADS * npair * sub), np.float32)
    for h in range(B_HEADS):
        bd[h * B_VAL_DIM:(h + 1) * B_VAL_DIM, h * B_KEY_DIM:(h + 1) * B_KEY_DIM] = 1.0
        hm[h * B_VAL_DIM:(h + 1) * B_VAL_DIM, h * B_VAL_DIM:(h + 1) * B_VAL_DIM] = 1.0
        hmk[h, h * B_KEY_DIM:(h + 1) * B_KEY_DIM] = 1.0
        hmv[h, h * B_VAL_DIM:(h + 1) * B_VAL_DIM] = 1.0
        for i in range(B_PER):
            for s in range(sub):
                row = (i * B_HEADS + h) * sub + s
                gt[row, s * hk + h * B_KEY_DIM:s * hk + (h + 1) * B_KEY_DIM] = 1.0
        for i in range(nsub):
            i4 = i % B_PER
            md[i * sub:(i + 1) * sub, (i4 * B_HEADS + h) * sub:(i4 * B_HEADS + h + 1) * sub] = 1.0
        for p, (i, j) in enumerate(B_PAIRS):
            mo[i * sub:(i + 1) * sub, (h * npair + p) * sub:(h * npair + p + 1) * sub] = 1.0
    return tri, gt, bd, hm, hmk, hmv, md, mo


def _hgrn_kernel(qb_ref, fb_ref, ib_ref, og_ref, lbl_ref, onorm_ref, tri_ref, gt_ref, bd_ref, hm_ref,
                 hmk_ref, hmv_ref, md_ref, mo_ref, o_ref, st, dstack, bsc, qsc, ksc, osc, *, layer):
    s = qb_ref.shape[0]
    c = B_CHUNK
    rows = B_ROWS
    hk = B_QK_WIDTH
    sub = B_SUB
    nsub = B_NSUB

    st[...] = jnp.zeros(st.shape, F32)
    lbl = lbl_ref[...]
    e = jnp.exp(lbl - jnp.max(lbl, axis=0, keepdims=True))
    lb = jnp.sum(e[0:layer + 1, :], axis=0, keepdims=True) / jnp.sum(e, axis=0, keepdims=True)
    tri = tri_ref[...]

    def block(bi, carry):
        r0 = pl.multiple_of(bi * rows, rows)
        qr = qb_ref[pl.ds(r0, rows), :]
        q = qr * jax.nn.sigmoid(qr) * (B_KEY_DIM ** -0.5)
        f = lb + (1.0 - lb) * jax.nn.sigmoid(fb_ref[pl.ds(r0, rows), :])
        g1, g2, g3 = _split3(jnp.log(f))
        bcum = _mm(tri, g1) + _mm(tri, g2) + _mm(tri, g3)
        bsc[...] = bcum
        qsc[...] = q
        ksc[...] = 1.0 - f
        trow = lax.broadcasted_iota(I32, (rows, 1), 0)

        def vrows(lo, n):
            return ib_ref[pl.ds(r0 + lo, n), :]

        def rows_of(ref, off):
            return jnp.concatenate(
                [jnp.broadcast_to(ref[i * sub + off:i * sub + off + 1, :], (sub, ref.shape[1])) for i in range(nsub)],
                axis=0)

        rho = jnp.concatenate(
            [jnp.zeros((sub, hk), F32) if i % B_PER == 0
             else jnp.broadcast_to(bsc[i * sub - 1:i * sub, :], (sub, hk)) for i in range(nsub)], axis=0)
        qt = (qsc[...] * jnp.exp(bsc[...] - rho)).astype(BF16)
        kst = jnp.concatenate(
            [ksc[j * sub:(j + 1) * sub, :] * jnp.exp(bsc[i * sub - 1:i * sub, :] - bsc[j * sub:(j + 1) * sub, :])
             for i, j in B_PAIRS], axis=0)
        vst = jnp.concatenate([vrows(j * sub, sub) for _, j in B_PAIRS], axis=0)
        kall = jnp.concatenate([kst * hmk_ref[h:h + 1, :] for h in range(B_HEADS)], axis=0).astype(BF16)
        vall = jnp.concatenate([vst * hmv_ref[h:h + 1, :] for h in range(B_HEADS)], axis=0).astype(BF16)
        s_off = _nt(qt, kall) * mo_ref[...]
        osc[...] = _mm(s_off.astype(BF16), vall)

        for sp in range(sub):
            ok = (trow & (sub - 1)) >= sp
            ex = jnp.exp(jnp.where(ok, bsc[...] - rows_of(bsc, sp), NEG))
            dv = (qsc[...] * rows_of(ksc, sp)) * ex
            dstack[:, sp * hk:(sp + 1) * hk] = dv.astype(BF16)
        s_dia = (_nt(dstack[...], gt_ref[...]) * md_ref[...]).astype(BF16)

        for cc in range(rows // c):
            lo, hi = cc * c, (cc + 1) * c
            vdia = jnp.concatenate(
                [vrows(lo + i * sub, sub) * hmv_ref[h:h + 1, :] for i in range(B_PER) for h in range(B_HEADS)],
                axis=0).astype(BF16)
            stv = st[...]
            qe = (qsc[lo:hi, :] * jnp.exp(bsc[lo:hi, :])).astype(BF16)
            osc[lo:hi, :] += _mm(s_dia[lo:hi, :], vdia) + _nt(qe, stv.astype(BF16))
            blast = bsc[hi - 1:hi, :]
            kd = ksc[lo:hi, :] * jnp.exp(blast - bsc[lo:hi, :])
            st[...] = stv * jnp.exp(blast) + _tn(vrows(lo, c).astype(BF16), kd.astype(BF16)) * bd_ref[...]

        o = osc[...]
        o2 = o * o
        o2h = o2.astype(BF16)
        o2l = (o2 - o2h.astype(F32)).astype(BF16)
        ms = (_mm(o2h, hm_ref[...]) + _mm(o2l, hm_ref[...])) * (1.0 / B_VAL_DIM)
        ogr = og_ref[pl.ds(r0, rows), :]
        y = o * lax.rsqrt(ms + EPS) * onorm_ref[...] * (ogr * jax.nn.sigmoid(ogr))
        o_ref[pl.ds(r0, rows), :] = y.astype(o_ref.dtype)
        return carry

    lax.fori_loop(0, s // rows, block, 0)


def _hgrn(qf, io, lb_logits, layer, out_norm, batch, seq):
    t = qf.shape[0]
    tri, gt, bd, hm, hmk, hmv, md, mo = _hgrn_consts()
    nslot = lb_logits.shape[0]
    const = lambda a, dt: jnp.asarray(a, dt)
    full = lambda shp: pl.BlockSpec(shp, lambda b: (0,) * len(shp))
    return pl.pallas_call(
        functools.partial(_hgrn_kernel, layer=layer),
        grid=(batch,),
        in_specs=[
            pl.BlockSpec((seq, B_QK_WIDTH), lambda b: (b, 0)),
            pl.BlockSpec((seq, B_QK_WIDTH), lambda b: (b, 1)),
            pl.BlockSpec((seq, B_WIDTH), lambda b: (b, 0)),
            pl.BlockSpec((seq, B_WIDTH), lambda b: (b, 1)),
            full((nslot, B_QK_WIDTH)),
            full((1, B_WIDTH)),
            full(tri.shape), full(gt.shape), full(bd.shape), full(hm.shape),
            full(hmk.shape), full(hmv.shape), full(md.shape), full(mo.shape),
        ],
        out_specs=pl.BlockSpec((seq, B_WIDTH), lambda b: (b, 0)),
        out_shape=jax.ShapeDtypeStruct((t, B_WIDTH), BF16),
        scratch_shapes=[
            pltpu.VMEM((B_WIDTH, B_QK_WIDTH), F32),
            pltpu.VMEM((B_ROWS, B_SUB * B_QK_WIDTH), BF16),
            pltpu.VMEM((B_ROWS, B_QK_WIDTH), F32),
            pltpu.VMEM((B_ROWS, B_QK_WIDTH), F32),
            pltpu.VMEM((B_ROWS, B_QK_WIDTH), F32),
            pltpu.VMEM((B_ROWS, B_WIDTH), F32),
        ],
        compiler_params=_params(("parallel",)),
        name="hgrn2",
    )(qf, qf, io, io, lb_logits.astype(F32), out_norm.astype(F32).reshape(1, B_WIDTH),
      const(tri, BF16), const(gt, BF16), const(bd, F32), const(hm, BF16),
      const(hmk, F32), const(hmv, F32), const(md, F32), const(mo, F32))


def _head_ms(t, masks):
    sq = t * t
    ms = jnp.zeros_like(t)
    for mk in masks:
        ms = jnp.where(mk, jnp.sum(jnp.where(mk, sq, 0.0), axis=-1, keepdims=True), ms)
    return ms * (1.0 / HEAD_DIM)


def _mem_kernel(mem_ref, mg_ref, wkv_ref, q_ref, qg_ref, kg_ref, o_ref, ksc, vsc):
    s = q_ref.shape[0]
    lane = lax.broadcasted_iota(I32, (1, C_WIDTH), 1)
    masks = [(lane >= h * HEAD_DIM) & (lane < (h + 1) * HEAD_DIM) for h in range(C_HEADS)]

    m = mem_ref[0]
    ms = jnp.mean(m * m, axis=-1, keepdims=True)
    mn = (m * lax.rsqrt(ms + EPS) * mg_ref[...]).astype(BF16)
    kv = _mm(mn, wkv_ref[...])
    km = kv[:, :C_WIDTH]
    k = km * lax.rsqrt(_head_ms(km, masks) + EPS) * kg_ref[...]
    for h in range(C_HEADS):
        ksc[h] = jnp.where(masks[h], k, 0.0).astype(BF16)
    vsc[...] = kv[:, C_WIDTH:].astype(BF16)

    rows = 256

    def body(c, carry):
        r0 = pl.multiple_of(c * rows, rows)
        q = q_ref[pl.ds(r0, rows), :].astype(F32)
        qn = (q * lax.rsqrt(_head_ms(q, masks) + EPS) * qg_ref[...]).astype(BF16)
        out = jnp.zeros((rows, C_WIDTH), F32)
        for h in range(C_HEADS):
            lg = _nt(qn, ksc[h])
            mx = jnp.max(lg, axis=-1, keepdims=True)
            pe = jnp.exp(lg - mx)
            l = jnp.sum(pe, axis=-1, keepdims=True)
            out = jnp.where(masks[h], _mm(pe.astype(BF16), vsc[...]) / l, out)
        o_ref[pl.ds(r0, rows), :] = out.astype(o_ref.dtype)
        return carry

    lax.fori_loop(0, s // rows, body, 0)


def _memory(mem, mem_g, wkv_bf16, qc, q_gain, k_gain, seq):
    batch, mlen, d = mem.shape
    t = qc.shape[0]
    qg = (jnp.tile(q_gain.astype(F32), C_HEADS) * HEAD_DIM ** -0.5).reshape(1, C_WIDTH)
    kg = jnp.tile(k_gain.astype(F32), C_HEADS).reshape(1, C_WIDTH)
    return pl.pallas_call(
        _mem_kernel,
        grid=(batch,),
        in_specs=[
            pl.BlockSpec((1, mlen, d), lambda b: (b, 0, 0)),
            pl.BlockSpec((1, d), lambda b: (0, 0)),
            pl.BlockSpec(wkv_bf16.shape, lambda b: (0, 0)),
            pl.BlockSpec((seq, C_WIDTH), lambda b: (b, 0)),
            pl.BlockSpec((1, C_WIDTH), lambda b: (0, 0)),
            pl.BlockSpec((1, C_WIDTH), lambda b: (0, 0)),
        ],
        out_specs=pl.BlockSpec((seq, C_WIDTH), lambda b: (b, 0)),
        out_shape=jax.ShapeDtypeStruct((t, C_WIDTH), BF16),
        scratch_shapes=[
            pltpu.VMEM((C_HEADS, mlen, C_WIDTH), BF16),
            pltpu.VMEM((mlen, C_WIDTH), BF16),
        ],
        compiler_params=_params(("parallel",)),
        name="mem_xattn",
    )(mem, mem_g.astype(F32).reshape(1, d), wkv_bf16, qc, qg, kg)


def _merge_kernel(x_ref, ya_ref, yb_ref, yc_ref, gate_ref, wa_ref, wb_ref, wc_ref, wo_ref, fg_ref,
                  wrh_ref, wrl_ref, wsu_ref, wsd_ref, x2_ref, up_ref, st_ref):
    d = x_ref.shape[1]
    mixed = jnp.zeros(x_ref.shape, F32)
    for i, (y_ref, w_ref) in enumerate(((ya_ref, wa_ref), (yb_ref, wb_ref), (yc_ref, wc_ref))):
        g = jax.nn.sigmoid(gate_ref[:, i * d:(i + 1) * d].astype(F32))
        mixed = mixed + g * _mm(y_ref[...], w_ref[...])
    x1 = x_ref[...] + _mm(mixed.astype(BF16), wo_ref[...])
    ms = jnp.mean(x1 * x1, axis=-1, keepdims=True)
    u = x1 * lax.rsqrt(ms + EPS) * fg_ref[...]
    ub = u.astype(BF16)
    ul = (u - ub.astype(F32)).astype(BF16)
    logits = _nt(wrh_ref[...], ub) + _nt(wrl_ref[...], ub) + _nt(wrh_ref[...], ul)
    st_ref[...] = jax.nn.sigmoid(logits)
    hid = _mm(ub, wsu_ref[...])
    nh = hid.shape[1] // 2
    act = (jax.nn.silu(hid[:, :nh]) * hid[:, nh:]).astype(BF16)
    x2_ref[...] = x1 + _mm(act, wsd_ref[...])
    up_ref[...] = ub


def _merge(x2d, ya, yb, yc, gates, wa, wb, wc, wo, fg, wr_hi, wr_lo, wsu, wsd):
    t, d = x2d.shape
    tm = TM_PROJ
    row = lambda w: pl.BlockSpec((tm, w), lambda i: (i, 0))
    full = lambda a: pl.BlockSpec(a.shape, lambda i: (0,) * a.ndim)
    fg2 = fg.astype(F32).reshape(1, d)
    return pl.pallas_call(
        _merge_kernel,
        grid=(t // tm,),
        in_specs=[row(d), row(A_WIDTH), row(B_WIDTH), row(C_WIDTH), row(3 * d),
                  full(wa), full(wb), full(wc), full(wo), full(fg2), full(wr_hi), full(wr_lo),
                  full(wsu), full(wsd)],
        out_specs=[row(d), row(d), pl.BlockSpec((N_EXPERTS, tm), lambda i: (0, i))],
        out_shape=[jax.ShapeDtypeStruct((t, d), F32),
                   jax.ShapeDtypeStruct((t, d), BF16),
                   jax.ShapeDtypeStruct((N_EXPERTS, t), F32)],
        compiler_params=_params(("parallel",)),
        name="merge_router_shared",
    )(x2d, ya, yb, yc, gates, wa, wb, wc, wo, fg2, wr_hi, wr_lo, wsu, wsd)


def _row_to_expert():
    per_group = N_EXPERTS // N_GROUPS
    rows = np.arange(N_EXPERTS)
    return (rows % N_GROUPS) * per_group + rows // N_GROUPS


def _route_kernel(s_ref, bias_ref, eidx_ref, ut_ref, ones_ref, ls_ref,
                  idx_ref, w_ref, rank_ref, wd_ref, seg_ref, loff_ref, car_ref, carry):
    tr = s_ref.shape[1]
    per_group = N_EXPERTS // N_GROUPS
    ng = N_GROUPS
    nrep = tr // LANES
    ninf = -jnp.inf

    @pl.when(pl.program_id(0) == 0)
    def _():
        carry[...] = jnp.zeros(carry.shape, F32)

    s_all = s_ref[...]
    bias = jnp.concatenate([bias_ref[...]] * nrep, axis=1)
    eidx = jnp.concatenate([eidx_ref[...]] * nrep, axis=1)
    sv = [s_all[i * ng:(i + 1) * ng, :] for i in range(per_group)]
    ev = [eidx[i * ng:(i + 1) * ng, :] for i in range(per_group)]
    cv = [sv[i] + bias[i * ng:(i + 1) * ng, :] for i in range(per_group)]

    def vmax(xs):
        out = xs[0]
        for x in xs[1:]:
            out = jnp.maximum(out, x)
        return out

    def vmin(xs):
        out = xs[0]
        for x in xs[1:]:
            out = jnp.minimum(out, x)
        return out

    m1 = vmax(cv)
    i1 = vmin([jnp.where(cv[i] == m1, i, per_group) for i in range(per_group)])
    m2 = vmax([jnp.where(i1 == i, ninf, cv[i]) for i in range(per_group)])
    gs = m1 + m2
    giota = lax.broadcasted_iota(I32, (ng, tr), 0)
    gmask = jnp.zeros((ng, tr), jnp.bool_)
    for _ in range(TOPK_GROUPS):
        mx = jnp.max(gs, axis=0, keepdims=True)
        pick = jnp.min(jnp.where(gs == mx, giota, ng), axis=0, keepdims=True)
        sel = giota == pick
        gmask = jnp.logical_or(gmask, sel)
        gs = jnp.where(sel, ninf, gs)

    cm = [jnp.where(gmask, cv[i], ninf) for i in range(per_group)]
    idx_out = jnp.zeros((TOP_K, tr), I32)
    w_out = jnp.zeros((TOP_K, tr), F32)
    jiota = lax.broadcasted_iota(I32, (TOP_K, tr), 0)
    for j in range(TOP_K):
        mx = jnp.max(vmax(cm), axis=0, keepdims=True)
        emin = jnp.min(vmin([jnp.where(cm[i] == mx, ev[i], N_EXPERTS) for i in range(per_group)]),
                       axis=0, keepdims=True)
        hit = [ev[i] == emin for i in range(per_group)]
        ssel = hit_sum = None
        for i in range(per_group):
            term = jnp.where(hit[i], sv[i], 0.0)
            hit_sum = term if hit_sum is None else hit_sum + term
        ssel = jnp.sum(hit_sum, axis=0, keepdims=True)
        cm = [jnp.where(hit[i], ninf, cm[i]) for i in range(per_group)]
        idx_out = jnp.where(jiota == j, emin, idx_out)
        w_out = jnp.where(jiota == j, ssel, w_out)
    idx_ref[...] = idx_out
    w_norm = ROUTED_SCALE * w_out / jnp.sum(w_out, axis=0, keepdims=True)
    w_ref[...] = w_norm

    chosen = [jnp.where(jnp.logical_and(gmask, cm[i] == ninf), 1.0, 0.0) for i in range(per_group)]
    mh = jnp.concatenate(chosen, axis=0).astype(BF16)
    cnt = _mm(mh, ones_ref[...])
    seg8 = ((cnt.astype(I32) + (SUBLANES - 1)) // SUBLANES) * SUBLANES
    seg8f = seg8.astype(F32)
    loff = _mm(ls_ref[...], seg8f.astype(BF16))
    rank = _mm(mh, ut_ref[...])
    rank_ref[...] = jnp.where(mh > 0, rank, -1.0).astype(BF16)
    wrows = []
    for i in range(per_group):
        acc = jnp.zeros((ng, tr), F32)
        for j in range(TOP_K):
            acc = jnp.where(ev[i] == idx_out[j:j + 1, :], w_norm[j:j + 1, :], acc)
        wrows.append(acc)
    wd_ref[...] = jnp.concatenate(wrows, axis=0).astype(BF16)
    cr = carry[...]
    seg_ref[0] = seg8
    loff_ref[0] = loff.astype(I32)
    car_ref[0] = cr.astype(I32)
    carry[...] = cr + seg8f


def _route(s_t, bias_rows, t):
    tr = TB_MOE
    nt = t // tr
    r2e = _row_to_expert()
    eidx = jnp.asarray(np.broadcast_to(r2e[:, None], (N_EXPERTS, LANES)).astype(np.int32))
    bias = jnp.broadcast_to(bias_rows.astype(F32)[:, None], (N_EXPERTS, LANES))
    ut = jnp.asarray(np.triu(np.ones((tr, tr), np.float32), k=1), BF16)
    ls = jnp.asarray(np.tril(np.ones((N_EXPERTS, N_EXPERTS), np.float32), k=-1), BF16)
    ones = jnp.ones((tr, LANES), BF16)
    full = lambda a: pl.BlockSpec(a.shape, lambda i: (0,) * a.ndim)
    tile = pl.BlockSpec((TOP_K, tr), lambda i: (0, i))
    dense = pl.BlockSpec((N_EXPERTS, tr), lambda i: (0, i))
    tab = pl.BlockSpec((1, N_EXPERTS, LANES), lambda i: (i, 0, 0))
    tab_shape = jax.ShapeDtypeStruct((nt, N_EXPERTS, LANES), I32)
    return pl.pallas_call(
        _route_kernel,
        grid=(nt,),
        in_specs=[pl.BlockSpec((N_EXPERTS, tr), lambda i: (0, i)), full(bias), full(eidx), full(ut), full(ones),
                  full(ls)],
        out_specs=[tile, tile, dense, dense, tab, tab, tab],
        out_shape=[jax.ShapeDtypeStruct((TOP_K, t), I32),
                   jax.ShapeDtypeStruct((TOP_K, t), F32),
                   jax.ShapeDtypeStruct((N_EXPERTS, t), BF16),
                   jax.ShapeDtypeStruct((N_EXPERTS, t), BF16),
                   tab_shape, tab_shape, tab_shape],
        scratch_shapes=[pltpu.VMEM((N_EXPERTS, LANES), F32)],
        compiler_params=_params(("arbitrary",)),
        name="route_topk_layout",
    )(s_t, bias, eidx, ut, ones, ls)


def _piece_copies(grow_ref, used_ref, tile, local_ref, hbm_ref, sem, to_hbm):
    base = tile * PIECES_MOE
    used = used_ref[tile]
    group = 4

    def piece(p, carry, queue=0):
        loc = local_ref.at[pl.ds(pl.multiple_of(p * SUBLANES, SUBLANES), SUBLANES)]
        glb = hbm_ref.at[pl.ds(pl.multiple_of(grow_ref[base + p], SUBLANES), SUBLANES)]
        if to_hbm:
            pltpu.make_async_copy(loc, glb, sem).start(priority=queue)
        else:
            pltpu.make_async_copy(glb, loc, sem).start(priority=queue)
        return carry

    def pieces(g, carry):
        for u in range(group):
            piece(g * group + u, carry, u % 2)
        return carry

    whole = used // group
    lax.fori_loop(0, whole, pieces, 0)
    lax.fori_loop(whole * group, used, piece, 0)


def _wait_rows(n8, local_ref, hbm_ref, sem):
    for bit in USED_BITS:
        @pl.when((n8 & bit) != 0)
        def _():
            size = SUBLANES * bit
            pltpu.make_async_copy(local_ref.at[pl.ds(0, size)], hbm_ref.at[pl.ds(0, size)], sem).wait()


def _chunk_match(rc, seg_ref, rank_ref):
    rows = lax.broadcasted_iota(I32, (RC_MOE, N_EXPERTS), 0) + rc * RC_MOE
    into = rows - seg_ref[0, 0:1, :]
    own = pltpu.bitcast(into, U32) < pltpu.bitcast(seg_ref[0, 1:2, :], U32)
    owner = jnp.where(own, 1.0, 0.0).astype(BF16)
    rel = jnp.sum(jnp.where(own, (into + 1).astype(F32), 0.0), axis=1, keepdims=True) - 1.0
    return owner, _mm(owner, rank_ref[...]) == rel


def _dispatch_kernel(grow_ref, used_ref, zrow_ref, nused_ref, seg_ref, rank_ref, u_ref, xs_hbm,
                     xloc, zbuf, sems, zsem):
    i = pl.program_id(0)
    nt = pl.num_programs(0)
    tb = u_ref.shape[0]
    bm = zbuf.shape[0]
    slot = lax.rem(i, 2)

    def zero_copy(e):
        return pltpu.make_async_copy(zbuf, xs_hbm.at[pl.ds(pl.multiple_of(zrow_ref[e], bm), bm)], zsem)

    @pl.when(i == 0)
    def _():
        zbuf[...] = jnp.zeros(zbuf.shape, zbuf.dtype)

        def zstart(e, carry):
            @pl.when(zrow_ref[e] >= 0)
            def _():
                zero_copy(e).start()
            return carry

        def zwait(e, carry):
            @pl.when(zrow_ref[e] >= 0)
            def _():
                zero_copy(e).wait()
            return carry

        lax.fori_loop(0, N_EXPERTS, zstart, 0)
        lax.fori_loop(0, N_EXPERTS, zwait, 0)

        def tail_copy(b):
            return pltpu.make_async_copy(zbuf, xs_hbm.at[pl.ds(pl.multiple_of(b * bm, bm), bm)], zsem)

        def tstart(b, carry):
            tail_copy(b).start()
            return carry

        def twait(b, carry):
            tail_copy(b).wait()
            return carry

        nblk = xs_hbm.shape[0] // bm
        lax.fori_loop(nused_ref[0], nblk, tstart, 0)
        lax.fori_loop(nused_ref[0], nblk, twait, 0)

    @pl.when(i >= 2)
    def _():
        _wait_rows(used_ref[i - 2], xloc.at[slot], xs_hbm, sems.at[slot])

    used = used_ref[i]
    ub = u_ref[...]
    def sort_chunk(rc):
        _, match = _chunk_match(rc, seg_ref, rank_ref)
        onehot = jnp.where(match, 1.0, 0.0).astype(BF16)
        xloc[slot, rc * RC_MOE:(rc + 1) * RC_MOE, :] = _pack_pairs(_mm(onehot, ub), True)

    for rc in range(ALWAYS_MOE):
        sort_chunk(rc)
    for rc in range(ALWAYS_MOE, RL_MOE // RC_MOE):
        pl.when(rc * (RC_MOE // SUBLANES) < used)(functools.partial(sort_chunk, rc))

    _piece_copies(grow_ref, used_ref, i, xloc.at[slot], xs_hbm, sems.at[slot], True)

    @pl.when(i == nt - 1)
    def _():
        _wait_rows(used, xloc.at[slot], xs_hbm, sems.at[slot])

        @pl.when(i >= 1)
        def _():
            _wait_rows(used_ref[i - 1], xloc.at[1 - slot], xs_hbm, sems.at[1 - slot])


def _dispatch(tables, zrow, n_used, seg_rows, rank_d, u_bf16, n_rows):
    t, d = u_bf16.shape
    tb = TB_MOE
    half = d // 2
    nsp = len(tables) + 2
    return pl.pallas_call(
        _dispatch_kernel,
        grid_spec=pltpu.PrefetchScalarGridSpec(
            num_scalar_prefetch=nsp,
            grid=(t // tb,),
            in_specs=[pl.BlockSpec((1,) + seg_rows.shape[1:], lambda i, *_: (i, 0, 0)),
                      pl.BlockSpec((N_EXPERTS, tb), lambda i, *_: (0, i)),
                      pl.BlockSpec((tb, d), lambda i, *_: (i, 0))],
            out_specs=pl.BlockSpec(memory_space=pl.ANY),
            scratch_shapes=[
                pltpu.VMEM((2, RL_MOE, half), U32),
                pltpu.VMEM((BM_EXPERT, half), U32),
                pltpu.SemaphoreType.DMA((2,)),
                pltpu.SemaphoreType.DMA,
            ],
        ),
        out_shape=jax.ShapeDtypeStruct((n_rows, half), U32),
        compiler_params=_params(("arbitrary",)),
        name="dispatch_rows",
    )(*tables, zrow, n_used, seg_rows, rank_d, u_bf16)


def _expert_kernel(be_ref, nu_ref, xs_ref, wu_ref, wd_ref, ys_ref):
    @pl.when(pl.program_id(0) < nu_ref[0])
    def _():
        xa, xb = _unpack_pairs(xs_ref[...])
        half = xa.shape[1]
        hid = _mm(xa.astype(BF16), wu_ref[0, :half, :]) + _mm(xb.astype(BF16), wu_ref[0, half:, :])
        nh = hid.shape[1] // 2
        act = (jax.nn.silu(hid[:, :nh]) * hid[:, nh:]).astype(BF16)
        ys_ref[...] = _pack_pairs(_mm(act, wd_ref[0]))

    @pl.when(pl.program_id(0) >= nu_ref[0])
    def _():
        ys_ref[...] = jnp.zeros(ys_ref.shape, ys_ref.dtype)


def _experts(block_expert, n_used, xs, wu, wd):
    n_rows, half = xs.shape
    bm = BM_EXPERT
    nblk = n_rows // bm
    d = 2 * half
    last = lambda b, nu: jnp.minimum(b, nu[0] - 1)
    return pl.pallas_call(
        _expert_kernel,
        grid_spec=pltpu.PrefetchScalarGridSpec(
            num_scalar_prefetch=2,
            grid=(nblk,),
            in_specs=[
                pl.BlockSpec((bm, half), lambda b, be, nu: (last(b, nu), 0)),
                pl.BlockSpec((1, d, wu.shape[2]), lambda b, be, nu: (be[last(b, nu)], 0, 0)),
                pl.BlockSpec((1, wd.shape[1], d), lambda b, be, nu: (be[last(b, nu)], 0, 0)),
            ],
            out_specs=pl.BlockSpec((bm, half), lambda b, be, nu: (b, 0)),
        ),
        out_shape=jax.ShapeDtypeStruct((n_rows, half), U32),
        compiler_params=_params(("arbitrary",)),
        name="expert_swiglu",
    )(block_expert, n_used, xs, wu, wd)


def _combine_kernel(grow_ref, used_ref, ys_hbm, seg_ref, rank_ref, wd_ref, x2_ref, o_ref,
                    yloc, acc, sems):
    i = pl.program_id(0)
    nt = pl.num_programs(0)
    tb = x2_ref.shape[0]
    half = yloc.shape[2]
    slot = lax.rem(i, 2)

    def fetch(tile, sl):
        _piece_copies(grow_ref, used_ref, tile, yloc.at[sl], ys_hbm, sems.at[sl], False)

    @pl.when(i == 0)
    def _():
        yloc[...] = jnp.zeros(yloc.shape, yloc.dtype)
        fetch(0, 0)

    @pl.when(i + 1 < nt)
    def _():
        fetch(i + 1, 1 - slot)

    used = used_ref[i]
    _wait_rows(used, yloc.at[slot], ys_hbm, sems.at[slot])

    def weights_of(kc):
        owner, match = _chunk_match(kc, seg_ref, rank_ref)
        return jnp.where(match, _mm(owner, wd_ref[...]), 0.0).astype(BF16)

    base = ALWAYS_MOE * RC_MOE
    pw = jnp.concatenate([weights_of(kc) for kc in range(ALWAYS_MOE)], axis=0)
    ya, yb = _unpack_pairs(yloc[slot, 0:base, :])
    acc[:, :half] = x2_ref[:, :half] + _tn(pw, ya.astype(BF16))
    acc[:, half:] = x2_ref[:, half:] + _tn(pw, yb.astype(BF16))
    for kc in range(ALWAYS_MOE, RL_MOE // RC_MOE):
        @pl.when(kc * (RC_MOE // SUBLANES) < used)
        def _():
            pwk = weights_of(kc)
            yak, ybk = _unpack_pairs(yloc[slot, kc * RC_MOE:(kc + 1) * RC_MOE, :])
            acc[:, :half] += _tn(pwk, yak.astype(BF16))
            acc[:, half:] += _tn(pwk, ybk.astype(BF16))
    o_ref[...] = acc[...]


def _combine(tables, ys, seg_rows, rank_d, w_d, x2):
    t, d = x2.shape
    tb = TB_MOE
    half = d // 2
    nsp = len(tables)
    tok = lambda w: pl.BlockSpec((tb, w), lambda i, *_: (i, 0))
    dense = pl.BlockSpec((N_EXPERTS, tb), lambda i, *_: (0, i))
    segs = pl.BlockSpec((1,) + seg_rows.shape[1:], lambda i, *_: (i, 0, 0))
    return pl.pallas_call(
        _combine_kernel,
        grid_spec=pltpu.PrefetchScalarGridSpec(
            num_scalar_prefetch=nsp,
            grid=(t // tb,),
            in_specs=[pl.BlockSpec(memory_space=pl.ANY), segs, dense, dense, tok(d)],
            out_specs=tok(d),
            scratch_shapes=[
                pltpu.VMEM((2, RL_MOE, half), U32),
                pltpu.VMEM((tb, d), F32),
                pltpu.SemaphoreType.DMA((2,)),
            ],
        ),
        out_shape=jax.ShapeDtypeStruct((t, d), F32),
        compiler_params=_params(("arbitrary",)),
        name="combine_rows",
    )(*tables, ys, seg_rows, rank_d, w_d, x2)


def kernel(x, mem, attn_norm_g, w_in, q_norm_a, k_norm_a, rel_bias, lb_logits, out_norm_b, mem_norm_g, w_mem_kv,
           q_norm_c, k_norm_c, w_branch_a, w_branch_b, w_branch_c, w_out, ffn_norm_g, w_router, router_bias,
           w_exp_up, w_exp_down, w_shared_up, w_shared_down):
    batch, seq, d = x.shape
    t = batch * seq
    depth = w_in.shape[0]
    tab = _bias_tables(rel_bias)
    r2e = _row_to_expert()
    xf = x.reshape(t, d)
    for layer in range(depth):
        qkv, qf, io, qc, gates = _inproj(xf, attn_norm_g[layer], w_in[layer].astype(BF16))
        ya = _attention(qkv, q_norm_a[layer], k_norm_a[layer], tab, batch, seq)
        yb = _hgrn(qf, io, lb_logits, layer, out_norm_b[layer], batch, seq)
        yc = _memory(mem, mem_norm_g[layer], w_mem_kv[layer].astype(BF16), qc, q_norm_c[layer], k_norm_c[layer], seq)

        wr_rows = w_router[layer].astype(F32).T[r2e]
        wr_hi = wr_rows.astype(BF16)
        wr_lo = (wr_rows - wr_hi.astype(F32)).astype(BF16)
        x2, u_bf16, s_t = _merge(
            xf, ya, yb, yc, gates,
            w_branch_a[layer].astype(BF16), w_branch_b[layer].astype(BF16), w_branch_c[layer].astype(BF16),
            w_out[layer].astype(BF16), ffn_norm_g[layer], wr_hi, wr_lo,
            w_shared_up[layer].astype(BF16), w_shared_down[layer].astype(BF16))

        _, _, rank_d, w_d, seg_tab, loff_tab, car_tab = _route(s_t, router_bias[layer][r2e], t)

        bm = BM_EXPERT
        nt = t // TB_MOE
        seg = seg_tab[:, :, 0]
        loff = loff_tab[:, :, 0]
        car = car_tab[:, :, 0]
        total = car[-1] + seg[-1]
        padded = (total + bm - 1) // bm * bm
        pend = jnp.cumsum(padded)
        pstart = pend - padded
        n_rows = (t * TOP_K + N_EXPERTS * (SUBLANES - 1) * nt) // bm * bm + N_EXPERTS * bm
        nblk = n_rows // bm
        block_row = jnp.arange(nblk, dtype=I32) * bm
        block_slot = jnp.minimum(jnp.sum((pend[None, :] <= block_row[:, None]).astype(I32), axis=1), N_EXPERTS - 1)
        block_expert = jnp.asarray(r2e, I32)[block_slot]
        n_used = (pend[-1:] // bm).astype(I32)
        zrow = jnp.where(padded > total, pend - bm, -1).astype(I32)
        seg_end = (loff + seg) // SUBLANES
        piece = jnp.arange(PIECES_MOE, dtype=I32)
        owner = jnp.minimum(jnp.sum((seg_end[:, None, :] <= piece[None, :, None]).astype(I32), axis=2),
                            N_EXPERTS - 1)
        shift = pstart[None, :] + car - loff
        is_owner = owner[:, :, None] == jnp.arange(N_EXPERTS, dtype=I32)[None, None, :]
        grow = jnp.sum(jnp.where(is_owner, shift[:, None, :], 0), axis=2) + piece[None, :] * SUBLANES
        tables = (grow.reshape(-1).astype(I32), seg_end[:, -1].astype(I32))

        seg_rows = jnp.zeros((nt, SUBLANES, N_EXPERTS), I32).at[:, 0, :].set(loff).at[:, 1, :].set(seg)

        xs = _dispatch(tables, zrow, n_used, seg_rows, rank_d, u_bf16, n_rows)
        ys = _experts(block_expert, n_used, xs, w_exp_up[layer].astype(BF16), w_exp_down[layer].astype(BF16))
        xf = _combine(tables, ys, seg_rows, rank_d, w_d, x2)
    return xf.reshape(batch, seq, d)
```
